```python
import jax, jax.numpy as jnp
from jax import lax
import numpy as np

D_MODEL = 1024
BATCH = 8
SEQ = 4096
DEPTH = 1
DEC_BATCH = 32
DEC_SEQ = 1
PAST_LEN = 16384
PAGE_SIZE = 128

MIX_WIDTH = D_MODEL
DN_HEADS = 4
DN_DK = 128
DN_DV = 128
DN_QK = DN_HEADS * DN_DK
DN_V = DN_HEADS * DN_DV
CONV_CH = 2 * DN_QK + DN_V
CONV_W = 4
DN_CHUNK = 64
AT_HEADS = 8
AT_HEAD_DIM = 64
AT_WIDTH = AT_HEADS * AT_HEAD_DIM
BRANCHES = ((128, 1), (512, 4), (2048, 16))
WINDOW = 2048
D_FF = 4 * D_MODEL
IN_SIZES = (CONV_CH, DN_V, DN_HEADS, DN_HEADS, 3 * AT_WIDTH)
IN_DIM = CONV_CH + DN_V + 2 * DN_HEADS + 3 * AT_WIDTH
EPS = 1e-6

kernel_name = 'hymba_gdn_dilated_alibi_step'


def rmsnorm(x, w):
    xf = x.astype(jnp.float32)
    y = xf * lax.rsqrt(jnp.mean(xf * xf, axis=-1, keepdims=True) + EPS) * w.astype(jnp.float32)
    return y.astype(x.dtype)


def l2norm(x):
    return x * lax.rsqrt(jnp.sum(x * x, axis=-1, keepdims=True) + EPS)


def alibi_slopes(n):
    return 2.0 ** (-8.0 * jnp.arange(1, n + 1, dtype=jnp.float32) / n)


def _to_chunks(a, nc):
    N, Tp, H = a.shape[:3]
    a = a.reshape((N, nc, DN_CHUNK, H) + a.shape[3:])
    return jnp.moveaxis(a, (1, 3), (0, 2))


def gated_delta(q, k, v, beta, g, state0):
    N, T, H, Dk = q.shape
    Dv = v.shape[-1]
    nc = -(-T // DN_CHUNK)
    Tp = nc * DN_CHUNK
    p4 = ((0, 0), (0, Tp - T), (0, 0), (0, 0))
    p3 = ((0, 0), (0, Tp - T), (0, 0))
    q = _to_chunks(jnp.pad(q, p4), nc)
    k = _to_chunks(jnp.pad(k, p4), nc)
    v = _to_chunks(jnp.pad(v, p4), nc)
    beta = _to_chunks(jnp.pad(beta, p3), nc)
    g = _to_chunks(jnp.pad(g, p3), nc)
    gc = jnp.cumsum(g, axis=-1)
    ci = jnp.arange(DN_CHUNK)
    incl = ci[:, None] >= ci[None, :]
    strict = ci[:, None] > ci[None, :]
    decay = jnp.exp(jnp.where(incl, gc[..., :, None] - gc[..., None, :], -jnp.inf))
    kb = k * beta[..., None]
    m = jnp.where(strict, jnp.einsum('...ik,...jk->...ij', kb, k) * decay, 0.0)
    a = m + jnp.eye(DN_CHUNK, dtype=m.dtype)
    rhs = jnp.concatenate([v * beta[..., None], kb * jnp.exp(gc)[..., None]], axis=-1)
    sol = lax.linalg.triangular_solve(a, rhs, left_side=True, lower=True, unit_diagonal=True)
    u, w = sol[..., :Dv], sol[..., Dv:]
    qk = jnp.where(incl, jnp.einsum('...ik,...jk->...ij', q, k) * decay, 0.0)

    def step(S, inp):
        qc, kc, uc, wc, qkc, gcc = inp
        v_new = uc - jnp.einsum('nhck,nhkv->nhcv', wc, S)
        o = (jnp.einsum('nhck,nhkv->nhcv', qc * jnp.exp(gcc)[..., None], S)
             + jnp.einsum('nhij,nhjv->nhiv', qkc, v_new))
        gl = gcc[..., -1]
        S = (S * jnp.exp(gl)[..., None, None]
             + jnp.einsum('nhck,nhcv->nhkv', kc * jnp.exp(gl[..., None] - gcc)[..., None], v_new))
        return S, o

    state, o = lax.scan(step, state0, (q, k, u, w, qk, gc))
    o = jnp.moveaxis(o, (0, 2), (1, 3)).reshape(N, Tp, H, Dv)[:, :T]
    return o, state


def banded_attention(q, k, v, slopes, dil, band):
    N, L, H, Dh = q.shape
    nb = L // band
    qb = q.reshape(N, nb, band, H, Dh)
    pad = ((0, 0), (band, 0), (0, 0), (0, 0))
    kp = jnp.pad(k, pad).reshape(N, nb + 1, band, H, Dh)
    vp = jnp.pad(v, pad).reshape(N, nb + 1, band, H, Dh)
    kw = jnp.concatenate([kp[:, :-1], kp[:, 1:]], axis=2)
    vw = jnp.concatenate([vp[:, :-1], vp[:, 1:]], axis=2)
    s = jnp.einsum('nbqhd,nbkhd->nbhqk', qb, kw).astype(jnp.float32) * (AT_HEAD_DIM ** -0.5)
    qi = jnp.arange(band)[:, None] + band
    kj = jnp.arange(2 * band)[None, :]
    dist = qi - kj
    blk = jnp.arange(nb)[:, None, None]
    mask = (dist >= 0)[None] & (dist <= band)[None] & (blk * band + kj[None] - band >= 0)
    bias = -slopes[:, None, None] * (dil * dist).astype(jnp.float32)[None]
    s = jnp.where(mask[None, :, None], s + bias[None, None], -jnp.inf)
    lse = jax.nn.logsumexp(s, axis=-1)
    p = jnp.exp(s - lse[..., None])
    o = jnp.einsum('nbhqk,nbkhd->nbqhd', p, vw.astype(jnp.float32))
    return o.reshape(N, L, H, Dh), jnp.swapaxes(lse, 2, 3).reshape(N, L, H)


def combine_branches(outs, lses):
    wts = jax.nn.softmax(jnp.stack(lses, axis=0), axis=0)
    return jnp.einsum('gnth,gnthd->nthd', wts, jnp.stack(outs, axis=0))


def dilated_attn_prompt(q, k, v, slopes):
    N, T, H, Dh = q.shape
    outs, lses = [], []
    for win, dil in BRANCHES:
        band = win // dil
        span = dil * band
        Tp = -(-T // span) * span
        Q = Tp // dil

        def to_res(a):
            a = jnp.pad(a, ((0, 0), (0, Tp - T), (0, 0), (0, 0)))
            return a.reshape(N, Q, dil, H, Dh).transpose(0, 2, 1, 3, 4).reshape(N * dil, Q, H, Dh)

        o, lse = banded_attention(to_res(q), to_res(k), to_res(v), slopes, dil, band)
        o = o.reshape(N, dil, Q, H, Dh).transpose(0, 2, 1, 3, 4).reshape(N, Tp, H, Dh)[:, :T]
        lse = lse.reshape(N, dil, Q, H).transpose(0, 2, 1, 3).reshape(N, Tp, H)[:, :T]
        outs.append(o)
        lses.append(lse)
    return combine_branches(outs, lses)


def dilated_attn_sample(q, kc, vc, L, slopes):
    T = q.shape[1]
    outs, lses = [], []
    for win, dil in BRANCHES:
        band = win // dil
        r = jnp.arange(band + 1)
        idx = L + jnp.arange(T)[:, None] - r[None, :] * dil
        valid = idx >= 0
        idxc = jnp.maximum(idx, 0)
        kg = kc[:, idxc]
        vg = vc[:, idxc]
        s = jnp.einsum('nthd,ntrhd->nthr', q, kg).astype(jnp.float32) * (AT_HEAD_DIM ** -0.5)
        s = s - slopes[:, None] * (r * dil).astype(jnp.float32)[None, :]
        s = jnp.where(valid[None, :, None, :], s, -jnp.inf)
        lse = jax.nn.logsumexp(s, axis=-1)
        p = jnp.exp(s - lse[..., None])
        outs.append(jnp.einsum('nthr,ntrhd->nthd', p, vg.astype(jnp.float32)))
        lses.append(lse)
    return combine_branches(outs, lses)


def mixer(h, conv_prev, ssm_prev, kbuf, vbuf, w_in, conv_w, a_log, dt_bias, dn_norm, w_out, prompt):
    N, T, _ = h.shape
    proj = h @ w_in
    offs = [int(o) for o in np.cumsum(IN_SIZES)[:-1]]
    dn_qkv, dn_z, dn_b, dn_a, at_qkv = jnp.split(proj, offs, axis=-1)

    xc = jnp.concatenate([conv_prev.astype(dn_qkv.dtype), dn_qkv], axis=1)
    y = xc[:, 0:T] * conv_w[0]
    for i in range(1, CONV_W):
        y = y + xc[:, i:i + T] * conv_w[i]
    y = jax.nn.silu(y).astype(jnp.float32)
    conv_new = xc[:, T:]
    qd, kd, vd = jnp.split(y, [DN_QK, 2 * DN_QK], axis=-1)
    qd = l2norm(qd.reshape(N, T, DN_HEADS, DN_DK)) * (DN_DK ** -0.5)
    kd = l2norm(kd.reshape(N, T, DN_HEADS, DN_DK))
    vd = vd.reshape(N, T, DN_HEADS, DN_DV)
    beta = jax.nn.sigmoid(dn_b.astype(jnp.float32))
    g = -jnp.exp(a_log.astype(jnp.float32)) * jax.nn.softplus(dn_a.astype(jnp.float32) + dt_bias.astype(jnp.float32))
    od, ssm_new = gated_delta(qd, kd, vd, beta, g, ssm_prev.astype(jnp.float32))
    od = od * lax.rsqrt(jnp.mean(od * od, axis=-1, keepdims=True) + EPS) * dn_norm.astype(jnp.float32)
    od = od * jax.nn.silu(dn_z.astype(jnp.float32).reshape(N, T, DN_HEADS, DN_DV))
    od = od.reshape(N, T, DN_V).astype(h.dtype)

    qa, ka, va = [a.reshape(N, T, AT_HEADS, AT_HEAD_DIM) for a in jnp.split(at_qkv, 3, axis=-1)]
    slopes = alibi_slopes(AT_HEADS)
    if prompt:
        oa = dilated_attn_prompt(qa, ka, va, slopes)
        keep = min(WINDOW, T)
        k_new, v_new = ka[:, T - keep:], va[:, T - keep:]
    else:
        L = kbuf.shape[1]
        kc = jnp.concatenate([kbuf.astype(ka.dtype), ka], axis=1)
        vc = jnp.concatenate([vbuf.astype(va.dtype), va], axis=1)
        oa = dilated_attn_sample(qa, kc, vc, L, slopes)
        keep = min(WINDOW, L + T)
        k_new, v_new = kc[:, L + T - keep:], vc[:, L + T - keep:]
    oa = oa.reshape(N, T, AT_WIDTH).astype(h.dtype)

    out = jnp.concatenate([od, oa], axis=-1) @ w_out
    return out, conv_new, ssm_new.astype(h.dtype), k_new, v_new


def forward_group(x, conv_st, ssm_st, k_st, v_st, ln_mix, w_in, dn_conv_w, dn_a_log, dn_dt_bias,
                  dn_norm, w_out, ln_ffn, w_ffn_up, w_ffn_down, ln_final, prompt):
    N = x.shape[0]
    convs, ssms, ks, vs = [], [], [], []
    for l in range(DEPTH):
        if prompt:
            conv_prev = jnp.zeros((N, CONV_W - 1, CONV_CH), x.dtype)
            ssm_prev = jnp.zeros((N, DN_HEADS, DN_DK, DN_DV), jnp.float32)
            kb = None
            vb = None
        else:
            conv_prev, ssm_prev, kb, vb = conv_st[l], ssm_st[l], k_st[l], v_st[l]
        h = rmsnorm(x, ln_mix[l])
        mix, c_new, s_new, k_new, v_new = mixer(h, conv_prev, ssm_prev, kb, vb, w_in[l], dn_conv_w[l],
                                                dn_a_log[l], dn_dt_bias[l], dn_norm[l], w_out[l], prompt)
        x = x + mix
        h = rmsnorm(x, ln_ffn[l])
        x = x + jnp.square(jax.nn.relu(h @ w_ffn_up[l])) @ w_ffn_down[l]
        convs.append(c_new)
        ssms.append(s_new)
        ks.append(k_new)
        vs.append(v_new)
    y = rmsnorm(x, ln_final)
    return y, jnp.stack(convs), jnp.stack(ssms), jnp.stack(ks), jnp.stack(vs)


def setup_inputs(seed: int = 0) -> dict:
    key = jax.random.key(seed)
    ks = jax.random.split(key, 18)
    f32 = jnp.float32
    nrm = jax.random.normal
    buf = min(WINDOW, PAST_LEN)
    dt = jnp.exp(jax.random.uniform(ks[10], (DEPTH, DN_HEADS), f32, minval=np.log(1e-3), maxval=np.log(1e-1)))
    return {
        'x_prompt': nrm(ks[0], (BATCH, SEQ, D_MODEL), f32),
        'x_sample': nrm(ks[1], (DEC_BATCH, DEC_SEQ, D_MODEL), f32),
        'state_conv': nrm(ks[2], (DEPTH, DEC_BATCH, CONV_W - 1, CONV_CH), f32),
        'state_ssm': 0.05 * nrm(ks[3], (DEPTH, DEC_BATCH, DN_HEADS, DN_DK, DN_DV), f32),
        'cache_win_k': nrm(ks[4], (DEPTH, DEC_BATCH, buf, AT_HEADS, AT_HEAD_DIM), f32),
        'cache_win_v': nrm(ks[5], (DEPTH, DEC_BATCH, buf, AT_HEADS, AT_HEAD_DIM), f32),
        'ln_mix': 1.0 + 0.02 * nrm(ks[6], (DEPTH, D_MODEL), f32),
        'w_in': nrm(ks[7], (DEPTH, D_MODEL, IN_DIM), f32) * D_MODEL ** -0.5,
        'dn_conv_w': nrm(ks[8], (DEPTH, CONV_W, CONV_CH), f32) * CONV_W ** -0.5,
        'dn_a_log': jnp.log(jax.random.uniform(ks[9], (DEPTH, DN_HEADS), f32, minval=1.0, maxval=16.0)),
        'dn_dt_bias': dt + jnp.log(-jnp.expm1(-dt)),
        'dn_norm': 1.0 + 0.02 * nrm(ks[11], (DEPTH, DN_DV), f32),
        'w_out': nrm(ks[12], (DEPTH, MIX_WIDTH, D_MODEL), f32) * MIX_WIDTH ** -0.5,
        'ln_ffn': 1.0 + 0.02 * nrm(ks[13], (DEPTH, D_MODEL), f32),
        'w_ffn_up': nrm(ks[14], (DEPTH, D_MODEL, D_FF), f32) * D_MODEL ** -0.5,
        'w_ffn_down': nrm(ks[15], (DEPTH, D_FF, D_MODEL), f32) * D_FF ** -0.5,
        'ln_final': 1.0 + 0.02 * nrm(ks[16], (D_MODEL,), f32),
    }


def reference(x_prompt, x_sample, state_conv, state_ssm, cache_win_k, cache_win_v, ln_mix, w_in,
              dn_conv_w, dn_a_log, dn_dt_bias, dn_norm, w_out, ln_ffn, w_ffn_up, w_ffn_down, ln_final):
    y_prompt, p_conv, p_ssm, p_win_k, p_win_v = forward_group(
        x_prompt, None, None, None, None, ln_mix, w_in, dn_conv_w, dn_a_log, dn_dt_bias, dn_norm,
        w_out, ln_ffn, w_ffn_up, w_ffn_down, ln_final, True)
    y_sample, s_conv, s_ssm, s_win_k, s_win_v = forward_group(
        x_sample, state_conv, state_ssm, cache_win_k, cache_win_v, ln_mix, w_in, dn_conv_w, dn_a_log,
        dn_dt_bias, dn_norm, w_out, ln_ffn, w_ffn_up, w_ffn_down, ln_final, False)
    return (y_prompt, y_sample, p_conv, p_ssm, p_win_k, p_win_v, s_conv, s_ssm, s_win_k, s_win_v)
```

```python
import functools

import jax
import jax.numpy as jnp
from jax import lax
from jax.experimental import pallas as pl
from jax.experimental.pallas import tpu as pltpu

F32 = jnp.float32
BF16 = jnp.bfloat16

D_MODEL = 1024
DN_HEADS = 4
DN_DK = 128
DN_DV = 128
DN_QK = DN_HEADS * DN_DK
DN_V = DN_HEADS * DN_DV
CONV_CH = 2 * DN_QK + DN_V
CONV_W = 4
CHUNK = 64
AT_HEADS = 8
AT_HEAD_DIM = 64
AT_WIDTH = AT_HEADS * AT_HEAD_DIM
BAND = 128
DILATIONS = (1, 4, 16)
WINDOW = 2048
D_FF = 4 * D_MODEL
EPS = 1e-6

LANES = 128
SUBLANES = 8
HALO = SUBLANES
GATE_B0 = 0
GATE_A0 = DN_HEADS
VMEM_LIMIT = 56 * 1024 * 1024


def _rmsnorm(x, w):
    return x * lax.rsqrt(jnp.mean(x * x, axis=-1, keepdims=True) + EPS) * w


def _sigmoid(x):
    return 1.0 / (1.0 + jnp.exp(-x))


def _softplus(x):
    return jnp.maximum(x, 0.0) + jnp.log(1.0 + jnp.exp(-jnp.abs(x)))


def _mm(a, b):
    return jnp.dot(a.astype(BF16), b.astype(BF16), preferred_element_type=F32)


def _mm_nt(a, b):
    return lax.dot_general(a.astype(BF16), b.astype(BF16), (((1,), (1,)), ((), ())),
                           preferred_element_type=F32)


def _mm_tn(a, b):
    return lax.dot_general(a.astype(BF16), b.astype(BF16), (((0,), (0,)), ((), ())),
                           preferred_element_type=F32)


def _mm_exact(a, b):
    return jnp.dot(a, b, precision=lax.Precision.HIGHEST, preferred_element_type=F32)


def _inproj_kernel(x_ref, ln_ref, wdn_ref, wz_ref, wat_ref, wg_ref, dn_ref, z_ref, at_ref, g_ref):
    h = _rmsnorm(x_ref[...], ln_ref[...]).astype(BF16)
    dn_ref[...] = jnp.dot(h, wdn_ref[...], preferred_element_type=F32)
    z_ref[...] = jnp.dot(h, wz_ref[...], preferred_element_type=F32)
    at_ref[...] = jnp.dot(h, wat_ref[...], preferred_element_type=F32)
    g_ref[...] = jnp.dot(h, wg_ref[...], preferred_element_type=F32)


def _inproj(x, ln, wdn, wz, wat, wg, tm):
    m = x.shape[0]
    const = lambda i: (0, 0)
    row = lambda i: (i, 0)
    return pl.pallas_call(
        _inproj_kernel,
        grid=(m // tm,),
        in_specs=[
            pl.BlockSpec((tm, D_MODEL), row),
            pl.BlockSpec((1, D_MODEL), const),
            pl.BlockSpec((D_MODEL, CONV_CH), const),
            pl.BlockSpec((D_MODEL, DN_V), const),
            pl.BlockSpec((D_MODEL, 3 * AT_WIDTH), const),
            pl.BlockSpec((D_MODEL, LANES), const),
        ],
        out_specs=[
            pl.BlockSpec((tm, CONV_CH), row),
            pl.BlockSpec((tm, DN_V), row),
            pl.BlockSpec((tm, 3 * AT_WIDTH), row),
            pl.BlockSpec((tm, LANES), row),
        ],
        out_shape=[
            jax.ShapeDtypeStruct((m, CONV_CH), F32),
            jax.ShapeDtypeStruct((m, DN_V), F32),
            jax.ShapeDtypeStruct((m, 3 * AT_WIDTH), F32),
            jax.ShapeDtypeStruct((m, LANES), F32),
        ],
        compiler_params=pltpu.CompilerParams(
            dimension_semantics=("arbitrary",), vmem_limit_bytes=VMEM_LIMIT),
        name="inproj",
    )(x, ln, wdn, wz, wat, wg)


def _unit_lower_inverse(m, eye, same16, same32):
    n1 = jnp.where(same16, m, 0.0)
    p = eye - n1
    n2 = _mm(n1, n1)
    p = p + _mm(p, n2)
    n4 = _mm(n2, n2)
    p = p + _mm(p, n4)
    n8 = _mm(n4, n4)
    p = p + _mm(p, n8)
    e1 = jnp.where(jnp.logical_and(same32, jnp.logical_not(same16)), m, 0.0)
    t = p - _mm(_mm(p, e1), p)
    e2 = jnp.where(same32, 0.0, m)
    return t - _mm(_mm(t, e2), t)


def _gdn_kernel(cur_ref, halo_ref, conv0_ref, z_ref, gate_ref, s0_ref, cw_ref, par_ref, nrm_ref,
                od_ref, s_ref, tail_ref, xc_ref, *, nc, n_tiles, t_valid):
    j = pl.program_id(1)
    tc = nc * CHUNK

    @pl.when(j == 0)
    def _():
        xc_ref[0:HALO, :] = conv0_ref[0]
        s_ref[...] = s0_ref[...]

    @pl.when(j > 0)
    def _():
        xc_ref[0:HALO, :] = halo_ref[...]

    xc_ref[HALO:HALO + tc, :] = cur_ref[...]

    y = xc_ref[HALO - 3:HALO - 3 + tc, :] * cw_ref[0:1, :]
    for i in range(1, CONV_W):
        y = y + xc_ref[HALO - 3 + i:HALO - 3 + i + tc, :] * cw_ref[i:i + 1, :]
    y = y * _sigmoid(y)

    gate = gate_ref[...]
    beta_all = _sigmoid(gate)
    g_all = -jnp.exp(par_ref[0:1, :]) * _softplus(gate + par_ref[1:2, :])

    masked = t_valid < n_tiles * tc
    if masked:
        row = lax.broadcasted_iota(jnp.int32, (tc, 1), 0) + j * tc
        live = row < t_valid
        y = jnp.where(live, y, 0.0)
        beta_all = jnp.where(live, beta_all, 0.0)
        g_all = jnp.where(live, g_all, 0.0)

        @pl.when(j == n_tiles - 1)
        def _():
            lo = t_valid - (n_tiles - 1) * tc
            tail_ref[0] = xc_ref[lo:lo + HALO, :]
    else:
        @pl.when(j == n_tiles - 1)
        def _():
            tail_ref[0] = xc_ref[tc:tc + HALO, :]

    ri = lax.broadcasted_iota(jnp.int32, (tc, tc), 0)
    ci = lax.broadcasted_iota(jnp.int32, (tc, tc), 1)
    lblk = jnp.logical_and((ri >> 6) == (ci >> 6), ci <= ri).astype(F32)
    gc_all = _mm_exact(lblk, g_all)
    gc_t = gc_all.T

    r64 = lax.broadcasted_iota(jnp.int32, (CHUNK, CHUNK), 0)
    c64 = lax.broadcasted_iota(jnp.int32, (CHUNK, CHUNK), 1)
    incl = c64 <= r64
    strict = c64 < r64
    eye = (c64 == r64).astype(F32)
    same16 = (r64 >> 4) == (c64 >> 4)
    same32 = (r64 >> 5) == (c64 >> 5)

    nrm = nrm_ref[...]
    z = z_ref[...]
    for h in range(DN_HEADS):
        s = s_ref[0, h]
        for c in range(nc):
            rs = slice(c * CHUNK, (c + 1) * CHUNK)
            qh = y[rs, h * DN_DK:(h + 1) * DN_DK]
            kh = y[rs, DN_QK + h * DN_DK:DN_QK + (h + 1) * DN_DK]
            vh = y[rs, 2 * DN_QK + h * DN_DV:2 * DN_QK + (h + 1) * DN_DV]
            qn = qh * lax.rsqrt(jnp.sum(qh * qh, axis=-1, keepdims=True) + EPS) * (DN_DK ** -0.5)
            kn = kh * lax.rsqrt(jnp.sum(kh * kh, axis=-1, keepdims=True) + EPS)
            beta = beta_all[rs, GATE_B0 + h:GATE_B0 + h + 1]
            gcc = gc_all[rs, GATE_A0 + h:GATE_A0 + h + 1]
            gcr = gc_t[GATE_A0 + h:GATE_A0 + h + 1, rs]
            gl = gc_all[(c + 1) * CHUNK - 1:(c + 1) * CHUNK, GATE_A0 + h:GATE_A0 + h + 1]
            decay = jnp.exp(jnp.minimum(gcc - gcr, 0.0))
            egc = jnp.exp(gcc)
            kb = kn * beta
            m = jnp.where(strict, _mm_nt(kb, kn) * decay, 0.0)
            t = _unit_lower_inverse(m, eye, same16, same32)
            rhs = jnp.concatenate([vh * beta, kb * egc], axis=1)
            sol = _mm(t, rhs)
            u = sol[:, :DN_DV]
            w = sol[:, DN_DV:]
            qk = jnp.where(incl, _mm_nt(qn, kn) * decay, 0.0)
            v_new = u - _mm(w, s)
            o = _mm(qn * egc, s) + _mm(qk, v_new)
            s = s * jnp.exp(gl) + _mm_tn(kn * jnp.exp(gl - gcc), v_new)
            zh = z[rs, h * DN_DV:(h + 1) * DN_DV]
            on = o * lax.rsqrt(jnp.mean(o * o, axis=-1, keepdims=True) + EPS) * nrm
            od_ref[rs, h * DN_DV:(h + 1) * DN_DV] = on * (zh * _sigmoid(zh))
        s_ref[0, h] = s


def _gdn(dn, z, gate, conv0, s0, cw, par, nrm, n, t_pad, t_valid, nc):
    tc = nc * CHUNK
    n_tiles = t_pad // tc
    hb = tc // HALO
    kern = functools.partial(_gdn_kernel, nc=nc, n_tiles=n_tiles, t_valid=t_valid)
    tile = lambda b, j: (b * n_tiles + j, 0)
    const = lambda b, j: (0, 0)
    return pl.pallas_call(
        kern,
        grid=(n, n_tiles),
        in_specs=[
            pl.BlockSpec((tc, CONV_CH), tile),
            pl.BlockSpec((HALO, CONV_CH), lambda b, j: (jnp.maximum((b * n_tiles + j) * hb - 1, 0), 0)),
            pl.BlockSpec((1, HALO, CONV_CH), lambda b, j: (b, 0, 0)),
            pl.BlockSpec((tc, DN_V), tile),
            pl.BlockSpec((tc, LANES), tile),
            pl.BlockSpec((1, DN_HEADS, DN_DK, DN_DV), lambda b, j: (b, 0, 0, 0)),
            pl.BlockSpec((SUBLANES, CONV_CH), const),
            pl.BlockSpec((SUBLANES, LANES), const),
            pl.BlockSpec((1, DN_DV), const),
        ],
        out_specs=[
            pl.BlockSpec((tc, DN_V), tile),
            pl.BlockSpec((1, DN_HEADS, DN_DK, DN_DV), lambda b, j: (b, 0, 0, 0)),
            pl.BlockSpec((1, HALO, CONV_CH), lambda b, j: (b, 0, 0)),
        ],
        out_shape=[
            jax.ShapeDtypeStruct((n * t_pad, DN_V), F32),
            jax.ShapeDtypeStruct((n, DN_HEADS, DN_DK, DN_DV), F32),
            jax.ShapeDtypeStruct((n, HALO, CONV_CH), F32),
        ],
        scratch_shapes=[pltpu.VMEM((HALO + tc, CONV_CH), F32)],
        compiler_params=pltpu.CompilerParams(
            dimension_semantics=("arbitrary", "arbitrary"), vmem_limit_bytes=VMEM_LIMIT),
        name="gdn",
    )(dn, dn, conv0, z, gate, s0, cw, par, nrm)


def _attn_block(q_ref, k_ref, v_ref, start, prev, first, dil, bias_a, bias_b, first_mask, is_a):
    def rows(ref, at):
        if dil == 1:
            return ref[pl.ds(at, BAND), :]
        return ref[pl.ds(at, BAND, stride=dil), :]

    q = rows(q_ref, start) * (AT_HEAD_DIM ** -0.5)
    k2 = jnp.concatenate([rows(k_ref, prev), rows(k_ref, start)], axis=0).astype(BF16)
    v2 = jnp.concatenate([rows(v_ref, prev), rows(v_ref, start)], axis=0).astype(BF16)
    edge = jnp.where(first, first_mask, 0.0)

    def head(qm, bias):
        s = _mm_nt(qm, k2) + bias + edge
        mx = jnp.max(s, axis=-1, keepdims=True)
        p = jnp.exp(s - mx)
        l = jnp.sum(p, axis=-1, keepdims=True)
        pv = jnp.dot(p.astype(BF16), v2, preferred_element_type=F32)
        return pv * (1.0 / l), mx + jnp.log(l)

    oa, la = head(jnp.where(is_a, q, 0.0), bias_a)
    ob, lb = head(jnp.where(is_a, 0.0, q), bias_b)
    return jnp.where(is_a, oa, ob), jnp.where(is_a, la, lb)


def _attn_kernel(q_ref, k_ref, v_ref, slope_ref, o_ref, o16_ref, l16_ref, o4_ref, l4_ref, *, t):
    is_a = lax.broadcasted_iota(jnp.int32, (BAND, LANES), 1) < AT_HEAD_DIM
    ri = lax.broadcasted_iota(jnp.int32, (BAND, 2 * BAND), 0)
    cj = lax.broadcasted_iota(jnp.int32, (BAND, 2 * BAND), 1)
    dist = ri + BAND - cj
    distf = dist.astype(F32)
    neg = jnp.where(jnp.logical_and(dist >= 0, dist <= BAND), 0.0, -jnp.inf)
    first_mask = jnp.where(cj < BAND, -jnp.inf, 0.0)
    slope_a = slope_ref[0, 0:1, 0:1]
    slope_b = slope_ref[0, 0:1, AT_HEAD_DIM:AT_HEAD_DIM + 1]

    for dil, on_ref, ls_ref in ((16, o16_ref, l16_ref), (4, o4_ref, l4_ref), (1, None, None)):
        nb = t // (dil * BAND)
        shift = nb.bit_length() - 1
        bias_a = neg - (slope_a * float(dil)) * distf
        bias_b = neg - (slope_b * float(dil)) * distf

        def body(i, carry, dil=dil, nb=nb, shift=shift, bias_a=bias_a, bias_b=bias_b,
                 on_ref=on_ref, ls_ref=ls_ref):
            r = i >> shift
            b = i & (nb - 1)
            start = dil * BAND * b + r
            prev = dil * BAND * jnp.maximum(b - 1, 0) + r
            o, lse = _attn_block(q_ref, k_ref, v_ref, start, prev, b == 0, dil,
                                 bias_a, bias_b, first_mask, is_a)
            if dil > 1:
                on_ref[pl.ds(start, BAND, stride=dil), :] = o
                ls_ref[pl.ds(start, BAND, stride=dil), :] = lse
            else:
                rows = pl.ds(start, BAND)
                l16 = l16_ref[rows, :]
                l4 = l4_ref[rows, :]
                mx = jnp.maximum(jnp.maximum(lse, l16), l4)
                w1 = jnp.exp(lse - mx)
                w16 = jnp.exp(l16 - mx)
                w4 = jnp.exp(l4 - mx)
                num = w1 * o + w16 * o16_ref[rows, :] + w4 * o4_ref[rows, :]
                o_ref[rows, :] = num * (1.0 / (w1 + w16 + w4))
            return carry

        lax.fori_loop(0, dil * nb, body, 0)


def _attn_prompt(at, slopes, n, t):
    kern = functools.partial(_attn_kernel, t=t)
    pairs = AT_HEADS // 2
    return pl.pallas_call(
        kern,
        grid=(n, pairs),
        in_specs=[
            pl.BlockSpec((t, LANES), lambda b, p: (b, p)),
            pl.BlockSpec((t, LANES), lambda b, p: (b, pairs + p)),
            pl.BlockSpec((t, LANES), lambda b, p: (b, 2 * pairs + p)),
            pl.BlockSpec((1, SUBLANES, LANES), lambda b, p: (p, 0, 0)),
        ],
        out_specs=pl.BlockSpec((t, LANES), lambda b, p: (b, p)),
        out_shape=jax.ShapeDtypeStruct((n * t, AT_WIDTH), F32),
        scratch_shapes=[pltpu.VMEM((t, LANES), F32) for _ in range(4)],
        compiler_params=pltpu.CompilerParams(
            dimension_semantics=("arbitrary", "arbitrary"), vmem_limit_bytes=VMEM_LIMIT),
        name="attn_prompt",
    )(at, at, at, slopes)


def _attn_sample_kernel(ck_ref, cv_ref, qkv_ref, slope_ref, o_ref, nk_ref, nv_ref, *, length):
    q = qkv_ref[0, :, 0:AT_WIDTH] * (AT_HEAD_DIM ** -0.5)
    k_new = qkv_ref[0, :, AT_WIDTH:2 * AT_WIDTH]
    v_new = qkv_ref[0, :, 2 * AT_WIDTH:3 * AT_WIDTH]

    nk_ref[0, 0:length - 1, :] = ck_ref[0, 1:length, :]
    nk_ref[0, length - 1:length, :] = k_new
    nv_ref[0, 0:length - 1, :] = cv_ref[0, 1:length, :]
    nv_ref[0, length - 1:length, :] = v_new

    head_of_lane = lax.broadcasted_iota(jnp.int32, (AT_HEADS, AT_WIDTH), 1) >> 6
    head_of_row = lax.broadcasted_iota(jnp.int32, (AT_HEADS, AT_WIDTH), 0)
    own = head_of_lane == head_of_row
    qm = jnp.where(own, q, 0.0)
    slope = slope_ref[:, 0:1]

    kc = ck_ref[0].astype(BF16)
    vc = cv_ref[0].astype(BF16)
    s = _mm_nt(qm, kc)
    s_new = _mm_nt(qm, jnp.broadcast_to(k_new, (SUBLANES, AT_WIDTH)))[:, 0:1]

    d = length - lax.broadcasted_iota(jnp.int32, (AT_HEADS, length), 1)
    cnt = jnp.zeros((AT_HEADS, length), F32)
    for dil in DILATIONS:
        hit = jnp.logical_and((d & (dil - 1)) == 0, d <= BAND * dil)
        cnt = cnt + hit.astype(F32)
    s = jnp.where(cnt > 0.0, s - slope * d.astype(F32), -jnp.inf)
    mx = jnp.maximum(jnp.max(s, axis=-1, keepdims=True), s_new)
    p = cnt * jnp.exp(s - mx)
    p_new = float(len(DILATIONS)) * jnp.exp(s_new - mx)
    l = jnp.sum(p, axis=-1, keepdims=True) + p_new
    acc = jnp.dot(p.astype(BF16), vc, preferred_element_type=F32)
    acc = acc + _mm(jnp.broadcast_to(p_new, (AT_HEADS, SUBLANES)),
                    jnp.where(lax.broadcasted_iota(jnp.int32, (SUBLANES, AT_WIDTH), 0) == 0,
                              jnp.broadcast_to(v_new, (SUBLANES, AT_WIDTH)), 0.0))
    out = jnp.where(own, acc * (1.0 / l), 0.0)
    o_ref[0] = jnp.sum(out, axis=0, keepdims=True)


def _attn_sample(ck, cv, qkv, slopes, n, length):
    kern = functools.partial(_attn_sample_kernel, length=length)
    cache = pl.BlockSpec((1, length, AT_WIDTH), lambda b: (b, 0, 0))
    return pl.pallas_call(
        kern,
        grid=(n,),
        in_specs=[
            cache,
            cache,
            pl.BlockSpec((1, 1, 3 * AT_WIDTH), lambda b: (b, 0, 0)),
            pl.BlockSpec((AT_HEADS, LANES), lambda b: (0, 0)),
        ],
        out_specs=[pl.BlockSpec((1, 1, AT_WIDTH), lambda b: (b, 0, 0)), cache, cache],
        out_shape=[
            jax.ShapeDtypeStruct((n, 1, AT_WIDTH), F32),
            jax.ShapeDtypeStruct((n, length, AT_WIDTH), F32),
            jax.ShapeDtypeStruct((n, length, AT_WIDTH), F32),
        ],
        compiler_params=pltpu.CompilerParams(
            dimension_semantics=("arbitrary",), vmem_limit_bytes=VMEM_LIMIT),
        name="attn_sample",
    )(ck, cv, qkv, slopes)


def _outffn_kernel(x_ref, od_ref, oa_ref, wod_ref, woa_ref, ln2_ref, wup_ref, wdown_ref, lnf_ref,
                   y_ref, *, ff_chunk):
    mix = (jnp.dot(od_ref[...].astype(BF16), wod_ref[...], preferred_element_type=F32)
           + jnp.dot(oa_ref[...].astype(BF16), woa_ref[...], preferred_element_type=F32))
    x1 = x_ref[...] + mix
    h = _rmsnorm(x1, ln2_ref[...]).astype(BF16)
    acc = jnp.zeros_like(x1)
    for c in range(D_FF // ff_chunk):
        cols = slice(c * ff_chunk, (c + 1) * ff_chunk)
        up = jnp.dot(h, wup_ref[:, cols], preferred_element_type=F32)
        act = jnp.square(jnp.maximum(up, 0.0)).astype(BF16)
        acc = acc + jnp.dot(act, wdown_ref[cols, :], preferred_element_type=F32)
    y_ref[...] = _rmsnorm(x1 + acc, lnf_ref[...])


def _outffn(x, od, oa, wod, woa, ln2, wup, wdown, lnf, tm):
    m = x.shape[0]
    row = lambda i: (i, 0)
    const = lambda i: (0, 0)
    resident = functools.partial(pl.BlockSpec, index_map=const, pipeline_mode=pl.Buffered(1))
    return pl.pallas_call(
        functools.partial(_outffn_kernel, ff_chunk=1024),
        grid=(m // tm,),
        in_specs=[
            pl.BlockSpec((tm, D_MODEL), row),
            pl.BlockSpec((tm, DN_V), row),
            pl.BlockSpec((tm, AT_WIDTH), row),
            resident((DN_V, D_MODEL)),
            resident((AT_WIDTH, D_MODEL)),
            resident((1, D_MODEL)),
            resident((D_MODEL, D_FF)),
            resident((D_FF, D_MODEL)),
            resident((1, D_MODEL)),
        ],
        out_specs=pl.BlockSpec((tm, D_MODEL), row),
        out_shape=jax.ShapeDtypeStruct((m, D_MODEL), F32),
        compiler_params=pltpu.CompilerParams(
            dimension_semantics=("arbitrary",), vmem_limit_bytes=VMEM_LIMIT),
        name="outffn",
    )(x, od, oa, wod, woa, ln2, wup, wdown, lnf)


def _layer_params(ln_mix, w_in, dn_conv_w, dn_a_log, dn_dt_bias, dn_norm, w_out, ln_ffn, w_ffn_up,
                  w_ffn_down, ln_final):
    w = w_in[0]
    o_z = CONV_CH
    o_b = o_z + DN_V
    o_at = o_b + 2 * DN_HEADS
    wg = jnp.zeros((D_MODEL, LANES), F32).at[:, :2 * DN_HEADS].set(w[:, o_b:o_at])
    par = jnp.zeros((SUBLANES, LANES), F32)
    par = par.at[0, GATE_A0:GATE_A0 + DN_HEADS].set(dn_a_log[0].astype(F32))
    par = par.at[1, GATE_A0:GATE_A0 + DN_HEADS].set(dn_dt_bias[0].astype(F32))
    cw = jnp.zeros((SUBLANES, CONV_CH), F32).at[:CONV_W].set(dn_conv_w[0])
    slopes = 2.0 ** (-8.0 * jnp.arange(1, AT_HEADS + 1, dtype=F32) / AT_HEADS)
    pair_slopes = jnp.broadcast_to(
        jnp.repeat(slopes.reshape(AT_HEADS // 2, 2), AT_HEAD_DIM, axis=1)[:, None, :],
        (AT_HEADS // 2, SUBLANES, LANES))
    head_slopes = jnp.broadcast_to(slopes[:, None], (AT_HEADS, LANES))
    return dict(
        ln_mix=ln_mix[0][None, :],
        wdn=w[:, :o_z].astype(BF16), wz=w[:, o_z:o_b].astype(BF16), wat=w[:, o_at:].astype(BF16),
        wg=wg.astype(BF16), cw=cw, par=par, nrm=dn_norm[0][None, :].astype(F32),
        pair_slopes=pair_slopes, head_slopes=head_slopes,
        wod=w_out[0][:DN_V].astype(BF16), woa=w_out[0][DN_V:].astype(BF16),
        ln_ffn=ln_ffn[0][None, :], wup=w_ffn_up[0].astype(BF16), wdown=w_ffn_down[0].astype(BF16),
        ln_final=ln_final[None, :])


def _prompt_group(x, p):
    n, t, _ = x.shape
    assert t % (DILATIONS[-1] * BAND) == 0
    x2 = x.reshape(n * t, D_MODEL)
    dn, z, at, gate = _inproj(x2, p["ln_mix"], p["wdn"], p["wz"], p["wat"], p["wg"], tm=512)
    conv0 = jnp.zeros((n, HALO, CONV_CH), F32)
    s0 = jnp.zeros((n, DN_HEADS, DN_DK, DN_DV), F32)
    od, ssm, tail = _gdn(dn, z, gate, conv0, s0, p["cw"], p["par"], p["nrm"], n, t, t, nc=4)
    oa = _attn_prompt(at, p["pair_slopes"], n, t)
    y = _outffn(x2, od, oa, p["wod"], p["woa"], p["ln_ffn"], p["wup"], p["wdown"], p["ln_final"],
                tm=512)
    keep = min(WINDOW, t)
    at3 = at.reshape(n, t, 3 * AT_WIDTH)
    win_k = at3[:, t - keep:, AT_WIDTH:2 * AT_WIDTH].reshape(n, keep, AT_HEADS, AT_HEAD_DIM)
    win_v = at3[:, t - keep:, 2 * AT_WIDTH:].reshape(n, keep, AT_HEADS, AT_HEAD_DIM)
    return (y.reshape(n, t, D_MODEL), tail[:, HALO - (CONV_W - 1):][None], ssm[None],
            win_k[None], win_v[None])


def _sample_group(x, conv_st, ssm_st, k_st, v_st, p):
    n, t, _ = x.shape
    assert t == 1
    length = k_st.shape[2]
    assert length == WINDOW
    x2 = x.reshape(n, D_MODEL)
    dn, z, at, gate = _inproj(x2, p["ln_mix"], p["wdn"], p["wz"], p["wat"], p["wg"], tm=n)

    def pad_rows(a):
        return jnp.pad(a[:, None, :], ((0, 0), (0, CHUNK - 1), (0, 0))).reshape(n * CHUNK, a.shape[-1])

    conv0 = jnp.pad(conv_st[0].astype(F32), ((0, 0), (HALO - (CONV_W - 1), 0), (0, 0)))
    od, ssm, tail = _gdn(pad_rows(dn), pad_rows(z), pad_rows(gate), conv0, ssm_st[0].astype(F32),
                         p["cw"], p["par"], p["nrm"], n, CHUNK, 1, nc=1)
    od = od.reshape(n, CHUNK, DN_V)[:, 0]
    oa, win_k, win_v = _attn_sample(k_st[0].reshape(n, length, AT_WIDTH),
                                    v_st[0].reshape(n, length, AT_WIDTH),
                                    at[:, None, :], p["head_slopes"], n, length)
    y = _outffn(x2, od, oa.reshape(n, AT_WIDTH), p["wod"], p["woa"], p["ln_ffn"], p["wup"],
                p["wdown"], p["ln_final"], tm=n)
    return (y.reshape(n, 1, D_MODEL), tail[:, HALO - (CONV_W - 1):][None], ssm[None],
            win_k.reshape(n, length, AT_HEADS, AT_HEAD_DIM)[None],
            win_v.reshape(n, length, AT_HEADS, AT_HEAD_DIM)[None])


def kernel(x_prompt, x_sample, state_conv, state_ssm, cache_win_k, cache_win_v, ln_mix, w_in,
           dn_conv_w, dn_a_log, dn_dt_bias, dn_norm, w_out, ln_ffn, w_ffn_up, w_ffn_down, ln_final):
    p = _layer_params(ln_mix, w_in, dn_conv_w, dn_a_log, dn_dt_bias, dn_norm, w_out, ln_ffn,
                      w_ffn_up, w_ffn_down, ln_final)
    y_p, p_conv, p_ssm, p_k, p_v = _prompt_group(x_prompt, p)
    y_s, s_conv, s_ssm, s_k, s_v = _sample_group(x_sample, state_conv, state_ssm, cache_win_k,
                                                 cache_win_v, p)
    return (y_p, y_s, p_conv, p_ssm, p_k, p_v, s_conv, s_ssm, s_k, s_v)
```

```python
import functools

import jax
import jax.numpy as jnp
from jax import lax
from jax.experimental import pallas as pl
from jax.experimental.pallas import tpu as pltpu

F32 = jnp.float32
BF16 = jnp.bfloat16

D_MODEL = 1024
DN_HEADS = 4
DN_DK = 128
DN_DV = 128
DN_QK = DN_HEADS * DN_DK
DN_V = DN_HEADS * DN_DV
CONV_CH = 2 * DN_QK + DN_V
CONV_W = 4
CHUNK = 64
AT_HEADS = 8
AT_HEAD_DIM = 64
AT_WIDTH = AT_HEADS * AT_HEAD_DIM
BAND = 128
DILATIONS = (1, 4, 16)
WINDOW = 2048
D_FF = 4 * D_MODEL
EPS = 1e-6

LANES = 128
SUBLANES = 8
HALO = SUBLANES
GATE_B0 = 0
GATE_A0 = DN_HEADS
VMEM_LIMIT = 56 * 1024 * 1024


def _rmsnorm(x, w):
    return x * lax.rsqrt(jnp.mean(x * x, axis=-1, keepdims=True) + EPS) * w


def _sigmoid(x):
    return 1.0 / (1.0 + jnp.exp(-x))


def _softplus(x):
    return jnp.maximum(x, 0.0) + jnp.log(1.0 + jnp.exp(-jnp.abs(x)))


def _mm(a, b):
    return jnp.dot(a.astype(BF16), b.astype(BF16), preferred_element_type=F32)


def _mm_nt(a, b):
    return lax.dot_general(a.astype(BF16), b.astype(BF16), (((1,), (1,)), ((), ())),
                           preferred_element_type=F32)


def _mm_tn(a, b):
    return lax.dot_general(a.astype(BF16), b.astype(BF16), (((0,), (0,)), ((), ())),
                           preferred_element_type=F32)


def _mm_exact(a, b):
    return jnp.dot(a, b, precision=lax.Precision.HIGHEST, preferred_element_type=F32)


def _inproj_kernel(x_ref, ln_ref, wdn_ref, wz_ref, wat_ref, wg_ref, dn_ref, z_ref, at_ref, g_ref):
    h = _rmsnorm(x_ref[...], ln_ref[...]).astype(BF16)
    dn_ref[...] = jnp.dot(h, wdn_ref[...], preferred_element_type=F32)
    z_ref[...] = jnp.dot(h, wz_ref[...], preferred_element_type=F32)
    at_ref[...] = jnp.dot(h, wat_ref[...], preferred_element_type=F32)
    g_ref[...] = jnp.dot(h, wg_ref[...], preferred_element_type=F32)


def _inproj(x, ln, wdn, wz, wat, wg, tm):
    m = x.shape[0]
    const = lambda i: (0, 0)
    row = lambda i: (i, 0)
    return pl.pallas_call(
        _inproj_kernel,
        grid=(m // tm,),
        in_specs=[
            pl.BlockSpec((tm, D_MODEL), row),
            pl.BlockSpec((1, D_MODEL), const),
            pl.BlockSpec((D_MODEL, CONV_CH), const),
            pl.BlockSpec((D_MODEL, DN_V), const),
            pl.BlockSpec((D_MODEL, 3 * AT_WIDTH), const),
            pl.BlockSpec((D_MODEL, LANES), const),
        ],
        out_specs=[
            pl.BlockSpec((tm, CONV_CH), row),
            pl.BlockSpec((tm, DN_V), row),
            pl.BlockSpec((tm, 3 * AT_WIDTH), row),
            pl.BlockSpec((tm, LANES), row),
        ],
        out_shape=[
            jax.ShapeDtypeStruct((m, CONV_CH), F32),
            jax.ShapeDtypeStruct((m, DN_V), F32),
            jax.ShapeDtypeStruct((m, 3 * AT_WIDTH), F32),
            jax.ShapeDtypeStruct((m, LANES), F32),
        ],
        compiler_params=pltpu.CompilerParams(
            dimension_semantics=("arbitrary",), vmem_limit_bytes=VMEM_LIMIT),
        name="inproj",
    )(x, ln, wdn, wz, wat, wg)


def _unit_lower_inverses(ms, eye, same16, same32):
    n = [jnp.where(same16, m, 0.0) for m in ms]
    p = [eye - a for a in n]
    n = [_mm(a, a) for a in n]
    for last in (False, False, True):
        pn = [_mm(a, b) for a, b in zip(p, n)]
        if not last:
            n = [_mm(a, a) for a in n]
        p = [a + b for a, b in zip(p, pn)]
    off16 = jnp.logical_and(same32, jnp.logical_not(same16))
    pe = [_mm(a, jnp.where(off16, m, 0.0)) for a, m in zip(p, ms)]
    t = [a - _mm(b, a) for a, b in zip(p, pe)]
    te = [_mm(a, jnp.where(same32, 0.0, m)) for a, m in zip(t, ms)]
    return [a - _mm(b, a) for a, b in zip(t, te)]


def _gdn_kernel(cur_ref, halo_ref, conv0_ref, z_ref, gate_ref, s0_ref, cw_ref, par_ref, nrm_ref,
                od_ref, s_ref, tail_ref, xc_ref, *, nc, n_tiles, t_valid):
    j = pl.program_id(1)
    tc = nc * CHUNK

    @pl.when(j == 0)
    def _():
        xc_ref[0:HALO, :] = conv0_ref[0]
        s_ref[...] = s0_ref[...]

    @pl.when(j > 0)
    def _():
        xc_ref[0:HALO, :] = halo_ref[...]

    xc_ref[HALO:HALO + tc, :] = cur_ref[...]

    y = xc_ref[HALO - 3:HALO - 3 + tc, :] * cw_ref[0:1, :]
    for i in range(1, CONV_W):
        y = y + xc_ref[HALO - 3 + i:HALO - 3 + i + tc, :] * cw_ref[i:i + 1, :]
    y = y * _sigmoid(y)

    gate = gate_ref[...]
    beta_all = _sigmoid(gate)
    g_all = -jnp.exp(par_ref[0:1, :]) * _softplus(gate + par_ref[1:2, :])

    masked = t_valid < n_tiles * tc
    if masked:
        row = lax.broadcasted_iota(jnp.int32, (tc, 1), 0) + j * tc
        live = row < t_valid
        y = jnp.where(live, y, 0.0)
        beta_all = jnp.where(live, beta_all, 0.0)
        g_all = jnp.where(live, g_all, 0.0)

        @pl.when(j == n_tiles - 1)
        def _():
            lo = t_valid - (n_tiles - 1) * tc
            tail_ref[0] = xc_ref[lo:lo + HALO, :]
    else:
        @pl.when(j == n_tiles - 1)
        def _():
            tail_ref[0] = xc_ref[tc:tc + HALO, :]

    ri = lax.broadcasted_iota(jnp.int32, (tc, tc), 0)
    ci = lax.broadcasted_iota(jnp.int32, (tc, tc), 1)
    lblk = jnp.logical_and((ri >> 6) == (ci >> 6), ci <= ri).astype(F32)
    gc_all = _mm_exact(lblk, g_all)
    gc_t = gc_all.T

    r64 = lax.broadcasted_iota(jnp.int32, (CHUNK, CHUNK), 0)
    c64 = lax.broadcasted_iota(jnp.int32, (CHUNK, CHUNK), 1)
    incl = c64 <= r64
    strict = c64 < r64
    eye = (c64 == r64).astype(F32)
    same16 = (r64 >> 4) == (c64 >> 4)
    same32 = (r64 >> 5) == (c64 >> 5)

    nrm = nrm_ref[...]
    z = z_ref[...]
    cells = [(c, h) for c in range(nc) for h in range(DN_HEADS)]
    qn, kn, kb, vb, gcc, egc, gl, decay = [], [], [], [], [], [], [], []
    for c, h in cells:
        rs = slice(c * CHUNK, (c + 1) * CHUNK)
        qh = y[rs, h * DN_DK:(h + 1) * DN_DK]
        kh = y[rs, DN_QK + h * DN_DK:DN_QK + (h + 1) * DN_DK]
        vh = y[rs, 2 * DN_QK + h * DN_DV:2 * DN_QK + (h + 1) * DN_DV]
        beta = beta_all[rs, GATE_B0 + h:GATE_B0 + h + 1]
        col = gc_all[rs, GATE_A0 + h:GATE_A0 + h + 1]
        row = gc_t[GATE_A0 + h:GATE_A0 + h + 1, rs]
        qn.append(qh * lax.rsqrt(jnp.sum(qh * qh, axis=-1, keepdims=True) + EPS) * (DN_DK ** -0.5))
        kn.append(kh * lax.rsqrt(jnp.sum(kh * kh, axis=-1, keepdims=True) + EPS))
        kb.append(kn[-1] * beta)
        vb.append(vh * beta)
        gcc.append(col)
        egc.append(jnp.exp(col))
        gl.append(gc_all[(c + 1) * CHUNK - 1:(c + 1) * CHUNK, GATE_A0 + h:GATE_A0 + h + 1])
        decay.append(jnp.exp(jnp.minimum(col - row, 0.0)))

    kk = [_mm_nt(a, b) for a, b in zip(kb, kn)]
    qk = [_mm_nt(a, b) for a, b in zip(qn, kn)]
    ms = [jnp.where(strict, a * d, 0.0) for a, d in zip(kk, decay)]
    qk = [jnp.where(incl, a * d, 0.0) for a, d in zip(qk, decay)]
    ts = _unit_lower_inverses(ms, eye, same16, same32)
    sol = [_mm(t, jnp.concatenate([a, b * e], axis=1))
           for t, a, b, e in zip(ts, vb, kb, egc)]

    s = [s_ref[0, h] for h in range(DN_HEADS)]
    for c in range(nc):
        ids = [c * DN_HEADS + h for h in range(DN_HEADS)]
        ws = [_mm(sol[i][:, DN_DV:], s[h]) for h, i in enumerate(ids)]
        qs = [_mm(qn[i] * egc[i], s[h]) for h, i in enumerate(ids)]
        v_new = [sol[i][:, :DN_DV] - a for i, a in zip(ids, ws)]
        intra = [_mm(qk[i], v) for i, v in zip(ids, v_new)]
        upd = [_mm_tn(kn[i] * jnp.exp(gl[i] - gcc[i]), v) for i, v in zip(ids, v_new)]
        s = [s[h] * jnp.exp(gl[i]) + upd[h] for h, i in enumerate(ids)]
        rs = slice(c * CHUNK, (c + 1) * CHUNK)
        for h in range(DN_HEADS):
            o = qs[h] + intra[h]
            zh = z[rs, h * DN_DV:(h + 1) * DN_DV]
            on = o * lax.rsqrt(jnp.mean(o * o, axis=-1, keepdims=True) + EPS) * nrm
            od_ref[rs, h * DN_DV:(h + 1) * DN_DV] = on * (zh * _sigmoid(zh))
    for h in range(DN_HEADS):
        s_ref[0, h] = s[h]


def _gdn(dn, z, gate, conv0, s0, cw, par, nrm, n, t_pad, t_valid, nc):
    tc = nc * CHUNK
    n_tiles = t_pad // tc
    hb = tc // HALO
    kern = functools.partial(_gdn_kernel, nc=nc, n_tiles=n_tiles, t_valid=t_valid)
    tile = lambda b, j: (b * n_tiles + j, 0)
    const = lambda b, j: (0, 0)
    return pl.pallas_call(
        kern,
        grid=(n, n_tiles),
        in_specs=[
            pl.BlockSpec((tc, CONV_CH), tile),
            pl.BlockSpec((HALO, CONV_CH), lambda b, j: (jnp.maximum((b * n_tiles + j) * hb - 1, 0), 0)),
            pl.BlockSpec((1, HALO, CONV_CH), lambda b, j: (b, 0, 0)),
            pl.BlockSpec((tc, DN_V), tile),
            pl.BlockSpec((tc, LANES), tile),
            pl.BlockSpec((1, DN_HEADS, DN_DK, DN_DV), lambda b, j: (b, 0, 0, 0)),
            pl.BlockSpec((SUBLANES, CONV_CH), const),
            pl.BlockSpec((SUBLANES, LANES), const),
            pl.BlockSpec((1, DN_DV), const),
        ],
        out_specs=[
            pl.BlockSpec((tc, DN_V), tile),
            pl.BlockSpec((1, DN_HEADS, DN_DK, DN_DV), lambda b, j: (b, 0, 0, 0)),
            pl.BlockSpec((1, HALO, CONV_CH), lambda b, j: (b, 0, 0)),
        ],
        out_shape=[
            jax.ShapeDtypeStruct((n * t_pad, DN_V), F32),
            jax.ShapeDtypeStruct((n, DN_HEADS, DN_DK, DN_DV), F32),
            jax.ShapeDtypeStruct((n, HALO, CONV_CH), F32),
        ],
        scratch_shapes=[pltpu.VMEM((HALO + tc, CONV_CH), F32)],
        compiler_params=pltpu.CompilerParams(
            dimension_semantics=("arbitrary", "arbitrary"), vmem_limit_bytes=VMEM_LIMIT),
        name="gdn",
    )(dn, dn, conv0, z, gate, s0, cw, par, nrm)


def _attn_blocks(q_ref, k_ref, v_ref, starts, prevs, firsts, dil, bias_a, bias_b, first_mask, is_a):
    def rows(ref, at):
        if dil == 1:
            return ref[pl.ds(at, BAND), :]
        return ref[pl.ds(at, BAND, stride=dil), :]

    qs = [rows(q_ref, a) * (AT_HEAD_DIM ** -0.5) for a in starts]
    k2 = [jnp.concatenate([rows(k_ref, p), rows(k_ref, a)], axis=0).astype(BF16)
          for a, p in zip(starts, prevs)]
    v2 = [jnp.concatenate([rows(v_ref, p), rows(v_ref, a)], axis=0).astype(BF16)
          for a, p in zip(starts, prevs)]
    edges = [jnp.where(f, first_mask, 0.0) for f in firsts]

    heads = [(i, a) for i in range(len(starts)) for a in (True, False)]
    s = [_mm_nt(jnp.where(is_a, qs[i], 0.0) if a else jnp.where(is_a, 0.0, qs[i]), k2[i])
         for i, a in heads]
    s = [x + (bias_a if a else bias_b) + edges[i] for x, (i, a) in zip(s, heads)]
    mx = [jnp.max(x, axis=-1, keepdims=True) for x in s]
    p = [jnp.exp(x - m) for x, m in zip(s, mx)]
    l = [jnp.sum(x, axis=-1, keepdims=True) for x in p]
    pv = [jnp.dot(x.astype(BF16), v2[i], preferred_element_type=F32)
          for x, (i, a) in zip(p, heads)]
    outs = []
    for i in range(len(starts)):
        a, b = 2 * i, 2 * i + 1
        o = jnp.where(is_a, pv[a] * (1.0 / l[a]), pv[b] * (1.0 / l[b]))
        lse = jnp.where(is_a, mx[a] + jnp.log(l[a]), mx[b] + jnp.log(l[b]))
        outs.append((o, lse))
    return outs


def _attn_kernel(q_ref, k_ref, v_ref, slope_ref, o_ref, o16_ref, l16_ref, o4_ref, l4_ref, *, t,
                 group):
    is_a = lax.broadcasted_iota(jnp.int32, (BAND, LANES), 1) < AT_HEAD_DIM
    ri = lax.broadcasted_iota(jnp.int32, (BAND, 2 * BAND), 0)
    cj = lax.broadcasted_iota(jnp.int32, (BAND, 2 * BAND), 1)
    dist = ri + BAND - cj
    distf = dist.astype(F32)
    neg = jnp.where(jnp.logical_and(dist >= 0, dist <= BAND), 0.0, -jnp.inf)
    first_mask = jnp.where(cj < BAND, -jnp.inf, 0.0)
    slope_a = slope_ref[0, 0:1, 0:1]
    slope_b = slope_ref[0, 0:1, AT_HEAD_DIM:AT_HEAD_DIM + 1]

    for dil, on_ref, ls_ref in ((16, o16_ref, l16_ref), (4, o4_ref, l4_ref), (1, None, None)):
        nb = t // (dil * BAND)
        shift = nb.bit_length() - 1
        bias_a = neg - (slope_a * float(dil)) * distf
        bias_b = neg - (slope_b * float(dil)) * distf

        def body(it, carry, dil=dil, nb=nb, shift=shift, bias_a=bias_a, bias_b=bias_b,
                 on_ref=on_ref, ls_ref=ls_ref):
            blocks = [it * group + g for g in range(group)]
            res = [i >> shift for i in blocks]
            blk = [i & (nb - 1) for i in blocks]
            starts = [dil * BAND * b + r for b, r in zip(blk, res)]
            prevs = [dil * BAND * jnp.maximum(b - 1, 0) + r for b, r in zip(blk, res)]
            outs = _attn_blocks(q_ref, k_ref, v_ref, starts, prevs, [b == 0 for b in blk], dil,
                                bias_a, bias_b, first_mask, is_a)
            for start, (o, lse) in zip(starts, outs):
                if dil > 1:
                    on_ref[pl.ds(start, BAND, stride=dil), :] = o
                    ls_ref[pl.ds(start, BAND, stride=dil), :] = lse
                else:
                    rows = pl.ds(start, BAND)
                    l16 = l16_ref[rows, :]
                    l4 = l4_ref[rows, :]
                    mx = jnp.maximum(jnp.maximum(lse, l16), l4)
                    w1 = jnp.exp(lse - mx)
                    w16 = jnp.exp(l16 - mx)
                    w4 = jnp.exp(l4 - mx)
                    num = w1 * o + w16 * o16_ref[rows, :] + w4 * o4_ref[rows, :]
                    o_ref[rows, :] = num * (1.0 / (w1 + w16 + w4))
            return carry

        lax.fori_loop(0, dil * nb // group, body, 0)


def _attn_prompt(at, slopes, n, t, group=4):
    assert (t // BAND) % group == 0
    kern = functools.partial(_attn_kernel, t=t, group=group)
    pairs = AT_HEADS // 2
    return pl.pallas_call(
        kern,
        grid=(n, pairs),
        in_specs=[
            pl.BlockSpec((t, LANES), lambda b, p: (b, p)),
            pl.BlockSpec((t, LANES), lambda b, p: (b, pairs + p)),
            pl.BlockSpec((t, LANES), lambda b, p: (b, 2 * pairs + p)),
            pl.BlockSpec((1, SUBLANES, LANES), lambda b, p: (p, 0, 0)),
        ],
        out_specs=pl.BlockSpec((t, LANES), lambda b, p: (b, p)),
        out_shape=jax.ShapeDtypeStruct((n * t, AT_WIDTH), F32),
        scratch_shapes=[pltpu.VMEM((t, LANES), F32) for _ in range(4)],
        compiler_params=pltpu.CompilerParams(
            dimension_semantics=("arbitrary", "arbitrary"), vmem_limit_bytes=VMEM_LIMIT),
        name="attn_prompt",
    )(at, at, at, slopes)


def _attn_sample_kernel(ck_ref, cv_ref, qkv_ref, slope_ref, o_ref, nk_ref, nv_ref, *, length):
    q = qkv_ref[0, :, 0:AT_WIDTH] * (AT_HEAD_DIM ** -0.5)
    k_new = qkv_ref[0, :, AT_WIDTH:2 * AT_WIDTH]
    v_new = qkv_ref[0, :, 2 * AT_WIDTH:3 * AT_WIDTH]

    nk_ref[0, 0:length - 1, :] = ck_ref[0, 1:length, :]
    nk_ref[0, length - 1:length, :] = k_new
    nv_ref[0, 0:length - 1, :] = cv_ref[0, 1:length, :]
    nv_ref[0, length - 1:length, :] = v_new

    head_of_lane = lax.broadcasted_iota(jnp.int32, (AT_HEADS, AT_WIDTH), 1) >> 6
    head_of_row = lax.broadcasted_iota(jnp.int32, (AT_HEADS, AT_WIDTH), 0)
    own = head_of_lane == head_of_row
    qm = jnp.where(own, q, 0.0)
    slope = slope_ref[:, 0:1]

    kc = ck_ref[0].astype(BF16)
    vc = cv_ref[0].astype(BF16)
    s = _mm_nt(qm, kc)
    s_new = _mm_nt(qm, jnp.broadcast_to(k_new, (SUBLANES, AT_WIDTH)))[:, 0:1]

    d = length - lax.broadcasted_iota(jnp.int32, (AT_HEADS, length), 1)
    cnt = jnp.zeros((AT_HEADS, length), F32)
    for dil in DILATIONS:
        hit = jnp.logical_and((d & (dil - 1)) == 0, d <= BAND * dil)
        cnt = cnt + hit.astype(F32)
    s = jnp.where(cnt > 0.0, s - slope * d.astype(F32), -jnp.inf)
    mx = jnp.maximum(jnp.max(s, axis=-1, keepdims=True), s_new)
    p = cnt * jnp.exp(s - mx)
    p_new = float(len(DILATIONS)) * jnp.exp(s_new - mx)
    l = jnp.sum(p, axis=-1, keepdims=True) + p_new
    acc = jnp.dot(p.astype(BF16), vc, preferred_element_type=F32)
    acc = acc + _mm(jnp.broadcast_to(p_new, (AT_HEADS, SUBLANES)),
                    jnp.where(lax.broadcasted_iota(jnp.int32, (SUBLANES, AT_WIDTH), 0) == 0,
                              jnp.broadcast_to(v_new, (SUBLANES, AT_WIDTH)), 0.0))
    out = jnp.where(own, acc * (1.0 / l), 0.0)
    o_ref[0] = jnp.sum(out, axis=0, keepdims=True)


def _attn_sample(ck, cv, qkv, slopes, n, length):
    kern = functools.partial(_attn_sample_kernel, length=length)
    cache = pl.BlockSpec((1, length, AT_WIDTH), lambda b: (b, 0, 0))
    return pl.pallas_call(
        kern,
        grid=(n,),
        in_specs=[
            cache,
            cache,
            pl.BlockSpec((1, 1, 3 * AT_WIDTH), lambda b: (b, 0, 0)),
            pl.BlockSpec((AT_HEADS, LANES), lambda b: (0, 0)),
        ],
        out_specs=[pl.BlockSpec((1, 1, AT_WIDTH), lambda b: (b, 0, 0)), cache, cache],
        out_shape=[
            jax.ShapeDtypeStruct((n, 1, AT_WIDTH), F32),
            jax.ShapeDtypeStruct((n, length, AT_WIDTH), F32),
            jax.ShapeDtypeStruct((n, length, AT_WIDTH), F32),
        ],
        compiler_params=pltpu.CompilerParams(
            dimension_semantics=("arbitrary",), vmem_limit_bytes=VMEM_LIMIT),
        name="attn_sample",
    )(ck, cv, qkv, slopes)


def _outffn_kernel(x_ref, od_ref, oa_ref, wod_ref, woa_ref, ln2_ref, wup_ref, wdown_ref, lnf_ref,
                   y_ref, *, ff_chunk):
    mix = (jnp.dot(od_ref[...].astype(BF16), wod_ref[...], preferred_element_type=F32)
           + jnp.dot(oa_ref[...].astype(BF16), woa_ref[...], preferred_element_type=F32))
    x1 = x_ref[...] + mix
    h = _rmsnorm(x1, ln2_ref[...]).astype(BF16)
    acc = jnp.zeros_like(x1)
    for c in range(D_FF // ff_chunk):
        cols = slice(c * ff_chunk, (c + 1) * ff_chunk)
        up = jnp.dot(h, wup_ref[:, cols], preferred_element_type=F32)
        act = jnp.square(jnp.maximum(up, 0.0)).astype(BF16)
        acc = acc + jnp.dot(act, wdown_ref[cols, :], preferred_element_type=F32)
    y_ref[...] = _rmsnorm(x1 + acc, lnf_ref[...])


def _outffn(x, od, oa, wod, woa, ln2, wup, wdown, lnf, tm):
    m = x.shape[0]
    row = lambda i: (i, 0)
    const = lambda i: (0, 0)
    resident = functools.partial(pl.BlockSpec, index_map=const, pipeline_mode=pl.Buffered(1))
    return pl.pallas_call(
        functools.partial(_outffn_kernel, ff_chunk=1024),
        grid=(m // tm,),
        in_specs=[
            pl.BlockSpec((tm, D_MODEL), row),
            pl.BlockSpec((tm, DN_V), row),
            pl.BlockSpec((tm, AT_WIDTH), row),
            resident((DN_V, D_MODEL)),
            resident((AT_WIDTH, D_MODEL)),
            resident((1, D_MODEL)),
            resident((D_MODEL, D_FF)),
            resident((D_FF, D_MODEL)),
            resident((1, D_MODEL)),
        ],
        out_specs=pl.BlockSpec((tm, D_MODEL), row),
        out_shape=jax.ShapeDtypeStruct((m, D_MODEL), F32),
        compiler_params=pltpu.CompilerParams(
            dimension_semantics=("arbitrary",), vmem_limit_bytes=VMEM_LIMIT),
        name="outffn",
    )(x, od, oa, wod, woa, ln2, wup, wdown, lnf)


def _layer_params(ln_mix, w_in, dn_conv_w, dn_a_log, dn_dt_bias, dn_norm, w_out, ln_ffn, w_ffn_up,
                  w_ffn_down, ln_final):
    w = w_in[0]
    o_z = CONV_CH
    o_b = o_z + DN_V
    o_at = o_b + 2 * DN_HEADS
    wg = jnp.zeros((D_MODEL, LANES), F32).at[:, :2 * DN_HEADS].set(w[:, o_b:o_at])
    par = jnp.zeros((SUBLANES, LANES), F32)
    par = par.at[0, GATE_A0:GATE_A0 + DN_HEADS].set(dn_a_log[0].astype(F32))
    par = par.at[1, GATE_A0:GATE_A0 + DN_HEADS].set(dn_dt_bias[0].astype(F32))
    cw = jnp.zeros((SUBLANES, CONV_CH), F32).at[:CONV_W].set(dn_conv_w[0])
    slopes = 2.0 ** (-8.0 * jnp.arange(1, AT_HEADS + 1, dtype=F32) / AT_HEADS)
    pair_slopes = jnp.broadcast_to(
        jnp.repeat(slopes.reshape(AT_HEADS // 2, 2), AT_HEAD_DIM, axis=1)[:, None, :],
        (AT_HEADS // 2, SUBLANES, LANES))
    head_slopes = jnp.broadcast_to(slopes[:, None], (AT_HEADS, LANES))
    return dict(
        ln_mix=ln_mix[0][None, :],
        wdn=w[:, :o_z].astype(BF16), wz=w[:, o_z:o_b].astype(BF16), wat=w[:, o_at:].astype(BF16),
        wg=wg.astype(BF16), cw=cw, par=par, nrm=dn_norm[0][None, :].astype(F32),
        pair_slopes=pair_slopes, head_slopes=head_slopes,
        wod=w_out[0][:DN_V].astype(BF16), woa=w_out[0][DN_V:].astype(BF16),
        ln_ffn=ln_ffn[0][None, :], wup=w_ffn_up[0].astype(BF16), wdown=w_ffn_down[0].astype(BF16),
        ln_final=ln_final[None, :])


def _prompt_group(x, p):
    n, t, _ = x.shape
    assert t % (DILATIONS[-1] * BAND) == 0
    x2 = x.reshape(n * t, D_MODEL)
    dn, z, at, gate = _inproj(x2, p["ln_mix"], p["wdn"], p["wz"], p["wat"], p["wg"], tm=512)
    conv0 = jnp.zeros((n, HALO, CONV_CH), F32)
    s0 = jnp.zeros((n, DN_HEADS, DN_DK, DN_DV), F32)
    od, ssm, tail = _gdn(dn, z, gate, conv0, s0, p["cw"], p["par"], p["nrm"], n, t, t, nc=4)
    oa = _attn_prompt(at, p["pair_slopes"], n, t)
    y = _outffn(x2, od, oa, p["wod"], p["woa"], p["ln_ffn"], p["wup"], p["wdown"], p["ln_final"],
                tm=512)
    keep = min(WINDOW, t)
    at3 = at.reshape(n, t, 3 * AT_WIDTH)
    win_k = at3[:, t - keep:, AT_WIDTH:2 * AT_WIDTH].reshape(n, keep, AT_HEADS, AT_HEAD_DIM)
    win_v = at3[:, t - keep:, 2 * AT_WIDTH:].reshape(n, keep, AT_HEADS, AT_HEAD_DIM)
    return (y.reshape(n, t, D_MODEL), tail[:, HALO - (CONV_W - 1):][None], ssm[None],
            win_k[None], win_v[None])


def _sample_group(x, conv_st, ssm_st, k_st, v_st, p):
    n, t, _ = x.shape
    assert t == 1
    length = k_st.shape[2]
    assert length == WINDOW
    x2 = x.reshape(n, D_MODEL)
    dn, z, at, gate = _inproj(x2, p["ln_mix"], p["wdn"], p["wz"], p["wat"], p["wg"], tm=n)

    def pad_rows(a):
        return jnp.pad(a[:, None, :], ((0, 0), (0, CHUNK - 1), (0, 0))).reshape(n * CHUNK, a.shape[-1])

    conv0 = jnp.pad(conv_st[0].astype(F32), ((0, 0), (HALO - (CONV_W - 1), 0), (0, 0)))
    od, ssm, tail = _gdn(pad_rows(dn), pad_rows(z), pad_rows(gate), conv0, ssm_st[0].astype(F32),
                         p["cw"], p["par"], p["nrm"], n, CHUNK, 1, nc=1)
    od = od.reshape(n, CHUNK, DN_V)[:, 0]
    oa, win_k, win_v = _attn_sample(k_st[0].reshape(n, length, AT_WIDTH),
                                    v_st[0].reshape(n, length, AT_WIDTH),
                                    at[:, None, :], p["head_slopes"], n, length)
    y = _outffn(x2, od, oa.reshape(n, AT_WIDTH), p["wod"], p["woa"], p["ln_ffn"], p["wup"],
                p["wdown"], p["ln_final"], tm=n)
    return (y.reshape(n, 1, D_MODEL), tail[:, HALO - (CONV_W - 1):][None], ssm[None],
            win_k.reshape(n, length, AT_HEADS, AT_HEAD_DIM)[None],
            win_v.reshape(n, length, AT_HEADS, AT_HEAD_DIM)[None])


def kernel(x_prompt, x_sample, state_conv, state_ssm, cache_win_k, cache_win_v, ln_mix, w_in,
           dn_conv_w, dn_a_log, dn_dt_bias, dn_norm, w_out, ln_ffn, w_ffn_up, w_ffn_down, ln_final):
    p = _layer_params(ln_mix, w_in, dn_conv_w, dn_a_log, dn_dt_bias, dn_norm, w_out, ln_ffn,
                      w_ffn_up, w_ffn_down, ln_final)
    y_p, p_conv, p_ssm, p_k, p_v = _prompt_group(x_prompt, p)
    y_s, s_conv, s_ssm, s_k, s_v = _sample_group(x_sample, state_conv, state_ssm, cache_win_k,
                                                 cache_win_v, p)
    return (y_p, y_s, p_conv, p_ssm, p_k, p_v, s_conv, s_ssm, s_k, s_v)
```

```python
import functools

import jax
import jax.numpy as jnp
from jax import lax
from jax.experimental import pallas as pl
from jax.experimental.pallas import tpu as pltpu

F32 = jnp.float32
BF16 = jnp.bfloat16

D_MODEL = 1024
DN_HEADS = 4
DN_DK = 128
DN_DV = 128
DN_QK = DN_HEADS * DN_DK
DN_V = DN_HEADS * DN_DV
CONV_CH = 2 * DN_QK + DN_V
CONV_W = 4
CHUNK = 64
AT_HEADS = 8
AT_HEAD_DIM = 64
AT_WIDTH = AT_HEADS * AT_HEAD_DIM
BAND = 128
DILATIONS = (1, 4, 16)
WINDOW = 2048
D_FF = 4 * D_MODEL
EPS = 1e-6

LANES = 128
SUBLANES = 8
HALO = SUBLANES
GATE_B0 = 0
GATE_A0 = DN_HEADS
VMEM_LIMIT = 56 * 1024 * 1024


def _rmsnorm(x, w):
    return x * lax.rsqrt(jnp.mean(x * x, axis=-1, keepdims=True) + EPS) * w


def _sigmoid(x):
    return 1.0 / (1.0 + jnp.exp(-x))


def _softplus(x):
    return jnp.maximum(x, 0.0) + jnp.log(1.0 + jnp.exp(-jnp.abs(x)))


def _mm(a, b):
    return jnp.dot(a.astype(BF16), b.astype(BF16), preferred_element_type=F32)


def _mm_nt(a, b):
    return lax.dot_general(a.astype(BF16), b.astype(BF16), (((1,), (1,)), ((), ())),
                           preferred_element_type=F32)


def _mm_tn(a, b):
    return lax.dot_general(a.astype(BF16), b.astype(BF16), (((0,), (0,)), ((), ())),
                           preferred_element_type=F32)


def _mm_exact(a, b):
    return jnp.dot(a, b, precision=lax.Precision.HIGHEST, preferred_element_type=F32)


def _store_mixer_inputs(conv, y_ref):
    y = conv * _sigmoid(conv)
    for blk in range(2 * DN_HEADS):
        cols = slice(blk * DN_DK, (blk + 1) * DN_DK)
        part = y[:, cols]
        part = part * lax.rsqrt(jnp.sum(part * part, axis=-1, keepdims=True) + EPS)
        if blk < DN_HEADS:
            part = part * (DN_DK ** -0.5)
        y_ref[:, cols] = part
    y_ref[:, 2 * DN_QK:] = y[:, 2 * DN_QK:]


def _store_gates(gate, par_ref, gb_ref):
    lane = lax.broadcasted_iota(jnp.int32, (1, LANES), 1)
    beta = _sigmoid(gate)
    g = -jnp.exp(par_ref[0:1, :]) * _softplus(gate + par_ref[1:2, :])
    gb_ref[...] = jnp.where(lane < GATE_A0, beta, g)


def _inproj_prompt_kernel(x_ref, ln_ref, wdn_ref, wz_ref, wat_ref, wg_ref, cw_ref, par_ref,
                          y_ref, zs_ref, gb_ref, at_ref, tail_ref, xc_ref, *, tm, seq_tiles):
    j = lax.rem(pl.program_id(0), seq_tiles)

    @pl.when(pl.program_id(0) == 0)
    def _():
        xc_ref[tm:tm + HALO, :] = jnp.zeros((HALO, CONV_CH), F32)

    h = _rmsnorm(x_ref[...], ln_ref[...]).astype(BF16)

    xc_ref[0:HALO, :] = jnp.where(j == 0, 0.0, xc_ref[tm:tm + HALO, :])
    xc_ref[HALO:HALO + tm, :] = jnp.dot(h, wdn_ref[...], preferred_element_type=F32)
    tail_ref[0] = xc_ref[tm:tm + HALO, :]

    lo = HALO - (CONV_W - 1)
    conv = xc_ref[lo:lo + tm, :] * cw_ref[0:1, :]
    for i in range(1, CONV_W):
        conv = conv + xc_ref[lo + i:lo + i + tm, :] * cw_ref[i:i + 1, :]
    _store_mixer_inputs(conv, y_ref)

    at_ref[...] = jnp.dot(h, wat_ref[...], preferred_element_type=F32)
    z = jnp.dot(h, wz_ref[...], preferred_element_type=F32)
    zs_ref[...] = z * _sigmoid(z)
    _store_gates(jnp.dot(h, wg_ref[...], preferred_element_type=F32), par_ref, gb_ref)


def _inproj_sample_kernel(x_ref, hist_ref, ln_ref, wdn_ref, wz_ref, wat_ref, wg_ref, cw_ref, par_ref,
                          y_ref, zs_ref, gb_ref, at_ref, tail_ref):
    h = _rmsnorm(x_ref[...], ln_ref[...]).astype(BF16)
    dn = jnp.dot(h, wdn_ref[...], preferred_element_type=F32)
    z = jnp.dot(h, wz_ref[...], preferred_element_type=F32)
    zs_ref[...] = z * _sigmoid(z)
    at_ref[...] = jnp.dot(h, wat_ref[...], preferred_element_type=F32)
    gate = jnp.dot(h, wg_ref[...], preferred_element_type=F32)

    conv = hist_ref[0] * cw_ref[0:1, :]
    for i in range(1, CONV_W - 1):
        conv = conv + hist_ref[i] * cw_ref[i:i + 1, :]
    conv = conv + dn * cw_ref[CONV_W - 1:CONV_W, :]
    for i in range(1, CONV_W - 1):
        tail_ref[i - 1] = hist_ref[i]
    tail_ref[CONV_W - 2] = dn
    _store_mixer_inputs(conv, y_ref)
    _store_gates(gate, par_ref, gb_ref)


def _weight_specs():
    const = lambda i: (0, 0)
    return [
        pl.BlockSpec((1, D_MODEL), const),
        pl.BlockSpec((D_MODEL, CONV_CH), const),
        pl.BlockSpec((D_MODEL, DN_V), const),
        pl.BlockSpec((D_MODEL, 3 * AT_WIDTH), const),
        pl.BlockSpec((D_MODEL, LANES), const),
        pl.BlockSpec((SUBLANES, CONV_CH), const),
        pl.BlockSpec((SUBLANES, LANES), const),
    ]


def _inproj_prompt(x, p, n, t, tm):
    m = n * t
    seq_tiles = t // tm
    row = lambda i: (i, 0)
    return pl.pallas_call(
        functools.partial(_inproj_prompt_kernel, tm=tm, seq_tiles=seq_tiles),
        grid=(m // tm,),
        in_specs=[pl.BlockSpec((tm, D_MODEL), row)] + _weight_specs(),
        out_specs=[
            pl.BlockSpec((tm, CONV_CH), row),
            pl.BlockSpec((tm, DN_V), row),
            pl.BlockSpec((tm, LANES), row),
            pl.BlockSpec((tm, 3 * AT_WIDTH), row),
            pl.BlockSpec((1, HALO, CONV_CH), lambda i: (i // seq_tiles, 0, 0)),
        ],
        out_shape=[
            jax.ShapeDtypeStruct((m, CONV_CH), F32),
            jax.ShapeDtypeStruct((m, DN_V), F32),
            jax.ShapeDtypeStruct((m, LANES), F32),
            jax.ShapeDtypeStruct((m, 3 * AT_WIDTH), F32),
            jax.ShapeDtypeStruct((n, HALO, CONV_CH), F32),
        ],
        scratch_shapes=[pltpu.VMEM((HALO + tm, CONV_CH), F32)],
        compiler_params=pltpu.CompilerParams(
            dimension_semantics=("arbitrary",), vmem_limit_bytes=VMEM_LIMIT),
        name="inproj_prompt",
    )(x, p["ln_mix"], p["wdn"], p["wz"], p["wat"], p["wg"], p["cw"], p["par"])


def _inproj_sample(x, hist, p):
    n = x.shape[0]
    full = lambda i: (0, 0)
    full3 = lambda i: (0, 0, 0)
    return pl.pallas_call(
        _inproj_sample_kernel,
        grid=(1,),
        in_specs=[pl.BlockSpec((n, D_MODEL), full),
                  pl.BlockSpec((CONV_W - 1, n, CONV_CH), full3)] + _weight_specs(),
        out_specs=[
            pl.BlockSpec((n, CONV_CH), full),
            pl.BlockSpec((n, DN_V), full),
            pl.BlockSpec((n, LANES), full),
            pl.BlockSpec((n, 3 * AT_WIDTH), full),
            pl.BlockSpec((CONV_W - 1, n, CONV_CH), full3),
        ],
        out_shape=[
            jax.ShapeDtypeStruct((n, CONV_CH), F32),
            jax.ShapeDtypeStruct((n, DN_V), F32),
            jax.ShapeDtypeStruct((n, LANES), F32),
            jax.ShapeDtypeStruct((n, 3 * AT_WIDTH), F32),
            jax.ShapeDtypeStruct((CONV_W - 1, n, CONV_CH), F32),
        ],
        compiler_params=pltpu.CompilerParams(
            dimension_semantics=("arbitrary",), vmem_limit_bytes=VMEM_LIMIT),
        name="inproj_sample",
    )(x, hist, p["ln_mix"], p["wdn"], p["wz"], p["wat"], p["wg"], p["cw"], p["par"])


def _unit_lower_inverses(ms, eye, same16, same32):
    n = [jnp.where(same16, m, 0.0) for m in ms]
    p = [eye - a for a in n]
    n = [_mm(a, a) for a in n]
    for last in (False, False, True):
        pn = [_mm(a, b) for a, b in zip(p, n)]
        if not last:
            n = [_mm(a, a) for a in n]
        p = [a + b for a, b in zip(p, pn)]
    off16 = jnp.logical_and(same32, jnp.logical_not(same16))
    pe = [_mm(a, jnp.where(off16, m, 0.0)) for a, m in zip(p, ms)]
    t = [a - _mm(b, a) for a, b in zip(p, pe)]
    te = [_mm(a, jnp.where(same32, 0.0, m)) for a, m in zip(t, ms)]
    return [a - _mm(b, a) for a, b in zip(t, te)]


def _gated_norm(o, zs, nrm):
    return o * lax.rsqrt(jnp.mean(o * o, axis=-1, keepdims=True) + EPS) * nrm * zs


def _gdn_kernel(y_ref, zs_ref, gb_ref, nrm_ref, od_ref, s_ref, *, nc):
    tc = nc * CHUNK

    @pl.when(pl.program_id(1) == 0)
    def _():
        s_ref[...] = jnp.zeros(s_ref.shape, F32)

    y = y_ref[...]
    zs = zs_ref[...]
    gb = gb_ref[...]

    ri = lax.broadcasted_iota(jnp.int32, (tc, tc), 0)
    ci = lax.broadcasted_iota(jnp.int32, (tc, tc), 1)
    lblk = jnp.logical_and((ri >> 6) == (ci >> 6), ci <= ri).astype(F32)
    gc_all = _mm_exact(lblk, gb)
    gc_t = gc_all.T

    r64 = lax.broadcasted_iota(jnp.int32, (CHUNK, CHUNK), 0)
    c64 = lax.broadcasted_iota(jnp.int32, (CHUNK, CHUNK), 1)
    incl = c64 <= r64
    strict = c64 < r64
    eye = (c64 == r64).astype(F32)
    same16 = (r64 >> 4) == (c64 >> 4)
    same32 = (r64 >> 5) == (c64 >> 5)
    nrm = nrm_ref[...]

    cells = [(c, h) for c in range(nc) for h in range(DN_HEADS)]
    qn, kn, kb, vb, gcc, egc, gl, decay = [], [], [], [], [], [], [], []
    for c, h in cells:
        rs = slice(c * CHUNK, (c + 1) * CHUNK)
        beta = gb[rs, GATE_B0 + h:GATE_B0 + h + 1]
        col = gc_all[rs, GATE_A0 + h:GATE_A0 + h + 1]
        row = gc_t[GATE_A0 + h:GATE_A0 + h + 1, rs]
        qn.append(y[rs, h * DN_DK:(h + 1) * DN_DK])
        kn.append(y[rs, DN_QK + h * DN_DK:DN_QK + (h + 1) * DN_DK])
        kb.append(kn[-1] * beta)
        vb.append(y[rs, 2 * DN_QK + h * DN_DV:2 * DN_QK + (h + 1) * DN_DV] * beta)
        gcc.append(col)
        egc.append(jnp.exp(col))
        gl.append(gc_all[(c + 1) * CHUNK - 1:(c + 1) * CHUNK, GATE_A0 + h:GATE_A0 + h + 1])
        decay.append(jnp.exp(jnp.minimum(col - row, 0.0)))

    kk = [_mm_nt(a, b) for a, b in zip(kb, kn)]
    qk = [_mm_nt(a, b) for a, b in zip(qn, kn)]
    ms = [jnp.where(strict, a * d, 0.0) for a, d in zip(kk, decay)]
    qk = [jnp.where(incl, a * d, 0.0) for a, d in zip(qk, decay)]
    ts = _unit_lower_inverses(ms, eye, same16, same32)
    sol = [_mm(t, jnp.concatenate([a, b * e], axis=1))
           for t, a, b, e in zip(ts, vb, kb, egc)]

    s = [s_ref[0, h] for h in range(DN_HEADS)]
    for c in range(nc):
        ids = [c * DN_HEADS + h for h in range(DN_HEADS)]
        ws = [_mm(sol[i][:, DN_DV:], s[h]) for h, i in enumerate(ids)]
        qs = [_mm(qn[i] * egc[i], s[h]) for h, i in enumerate(ids)]
        v_new = [sol[i][:, :DN_DV] - a for i, a in zip(ids, ws)]
        intra = [_mm(qk[i], v) for i, v in zip(ids, v_new)]
        upd = [_mm_tn(kn[i] * jnp.exp(gl[i] - gcc[i]), v) for i, v in zip(ids, v_new)]
        s = [s[h] * jnp.exp(gl[i]) + upd[h] for h, i in enumerate(ids)]
        rs = slice(c * CHUNK, (c + 1) * CHUNK)
        for h in range(DN_HEADS):
            cols = slice(h * DN_DV, (h + 1) * DN_DV)
            od_ref[rs, cols] = _gated_norm(qs[h] + intra[h], zs[rs, cols], nrm)
    for h in range(DN_HEADS):
        s_ref[0, h] = s[h]


def _gdn_prompt(y, zs, gb, nrm, n, t, nc):
    tc = nc * CHUNK
    n_tiles = t // tc
    tile = lambda b, j: (b * n_tiles + j, 0)
    state = pl.BlockSpec((1, DN_HEADS, DN_DK, DN_DV), lambda b, j: (b, 0, 0, 0))
    return pl.pallas_call(
        functools.partial(_gdn_kernel, nc=nc),
        grid=(n, n_tiles),
        in_specs=[
            pl.BlockSpec((tc, CONV_CH), tile),
            pl.BlockSpec((tc, DN_V), tile),
            pl.BlockSpec((tc, LANES), tile),
            pl.BlockSpec((1, DN_DV), lambda b, j: (0, 0)),
        ],
        out_specs=[pl.BlockSpec((tc, DN_V), tile), state],
        out_shape=[
            jax.ShapeDtypeStruct((n * t, DN_V), F32),
            jax.ShapeDtypeStruct((n, DN_HEADS, DN_DK, DN_DV), F32),
        ],
        compiler_params=pltpu.CompilerParams(
            dimension_semantics=("arbitrary", "arbitrary"), vmem_limit_bytes=VMEM_LIMIT),
        name="gdn_prompt",
    )(y, zs, gb, nrm)


def _gdn_step_kernel(y_ref, zs_ref, gb_ref, s0_ref, nrm_ref, od_ref, s_ref, *, nb):
    y = y_ref[...]
    zs = zs_ref[...]
    gb = gb_ref[...]
    nrm = nrm_ref[...]
    rowid = lax.broadcasted_iota(jnp.int32, (nb, DN_DK), 0)
    cells = [(b, h) for b in range(nb) for h in range(DN_HEADS)]
    qh = [y[:, h * DN_DK:(h + 1) * DN_DK] for h in range(DN_HEADS)]
    kh = [y[:, DN_QK + h * DN_DK:DN_QK + (h + 1) * DN_DK] for h in range(DN_HEADS)]
    vh = [y[:, 2 * DN_QK + h * DN_DV:2 * DN_QK + (h + 1) * DN_DV] for h in range(DN_HEADS)]
    ks = [_mm(kh[h], s0_ref[b, h])[b:b + 1] for b, h in cells]
    qs = [_mm(qh[h], s0_ref[b, h])[b:b + 1] for b, h in cells]
    rs, egs = [], []
    for i, (b, h) in enumerate(cells):
        beta = gb[b:b + 1, GATE_B0 + h:GATE_B0 + h + 1]
        eg = jnp.exp(gb[b:b + 1, GATE_A0 + h:GATE_A0 + h + 1])
        rs.append(beta * (vh[h][b:b + 1] - eg * ks[i]))
        egs.append(eg)
    upd = [_mm_tn(jnp.where(rowid == b, kh[h], 0.0), jnp.broadcast_to(rs[i], (nb, DN_DV)))
           for i, (b, h) in enumerate(cells)]
    for i, (b, h) in enumerate(cells):
        s_ref[b, h] = s0_ref[b, h] * egs[i] + upd[i]
        qk = jnp.sum(qh[h][b:b + 1] * kh[h][b:b + 1], axis=-1, keepdims=True)
        cols = slice(h * DN_DV, (h + 1) * DN_DV)
        od_ref[b:b + 1, cols] = _gated_norm(egs[i] * qs[i] + qk * rs[i], zs[b:b + 1, cols], nrm)


def _gdn_step(y, zs, gb, s0, nrm, nb):
    n = y.shape[0]
    assert n % nb == 0
    row = lambda i: (i, 0)
    state = pl.BlockSpec((nb, DN_HEADS, DN_DK, DN_DV), lambda i: (i, 0, 0, 0))
    return pl.pallas_call(
        functools.partial(_gdn_step_kernel, nb=nb),
        grid=(n // nb,),
        in_specs=[
            pl.BlockSpec((nb, CONV_CH), row),
            pl.BlockSpec((nb, DN_V), row),
            pl.BlockSpec((nb, LANES), row),
            state,
            pl.BlockSpec((1, DN_DV), lambda i: (0, 0)),
        ],
        out_specs=[pl.BlockSpec((nb, DN_V), row), state],
        out_shape=[
            jax.ShapeDtypeStruct((n, DN_V), F32),
            jax.ShapeDtypeStruct((n, DN_HEADS, DN_DK, DN_DV), F32),
        ],
        compiler_params=pltpu.CompilerParams(
            dimension_semantics=("arbitrary",), vmem_limit_bytes=VMEM_LIMIT),
        name="gdn_step",
    )(y, zs, gb, s0, nrm)


def _attn_blocks(q_ref, k_ref, v_ref, starts, prevs, firsts, dil, bias_a, bias_b, first_mask, is_a):
    def rows(ref, at):
        if dil == 1:
            return ref[pl.ds(at, BAND), :]
        return ref[pl.ds(at, BAND, stride=dil), :]

    qs = [rows(q_ref, a) * (AT_HEAD_DIM ** -0.5) for a in starts]
    k2 = [jnp.concatenate([rows(k_ref, p), rows(k_ref, a)], axis=0).astype(BF16)
          for a, p in zip(starts, prevs)]
    v2 = [jnp.concatenate([rows(v_ref, p), rows(v_ref, a)], axis=0).astype(BF16)
          for a, p in zip(starts, prevs)]
    edges = [jnp.where(f, first_mask, 0.0) for f in firsts]

    heads = [(i, a) for i in range(len(starts)) for a in (True, False)]
    s = [_mm_nt(jnp.where(is_a, qs[i], 0.0) if a else jnp.where(is_a, 0.0, qs[i]), k2[i])
         for i, a in heads]
    s = [x + (bias_a if a else bias_b) + edges[i] for x, (i, a) in zip(s, heads)]
    mx = [jnp.max(x, axis=-1, keepdims=True) for x in s]
    p = [jnp.exp(x - m) for x, m in zip(s, mx)]
    l = [jnp.sum(x, axis=-1, keepdims=True) for x in p]
    pv = [jnp.dot(x.astype(BF16), v2[i], preferred_element_type=F32)
          for x, (i, a) in zip(p, heads)]
    outs = []
    for i in range(len(starts)):
        a, b = 2 * i, 2 * i + 1
        o = jnp.where(is_a, pv[a] * (1.0 / l[a]), pv[b] * (1.0 / l[b]))
        lse = jnp.where(is_a, mx[a] + jnp.log(l[a]), mx[b] + jnp.log(l[b]))
        outs.append((o, lse))
    return outs


def _attn_kernel(q_ref, k_ref, v_ref, slope_ref, o_ref, o16_ref, l16_ref, o4_ref, l4_ref, *, t,
                 group):
    is_a = lax.broadcasted_iota(jnp.int32, (BAND, LANES), 1) < AT_HEAD_DIM
    ri = lax.broadcasted_iota(jnp.int32, (BAND, 2 * BAND), 0)
    cj = lax.broadcasted_iota(jnp.int32, (BAND, 2 * BAND), 1)
    dist = ri + BAND - cj
    distf = dist.astype(F32)
    neg = jnp.where(jnp.logical_and(dist >= 0, dist <= BAND), 0.0, -jnp.inf)
    first_mask = jnp.where(cj < BAND, -jnp.inf, 0.0)
    slope_a = slope_ref[0, 0:1, 0:1]
    slope_b = slope_ref[0, 0:1, AT_HEAD_DIM:AT_HEAD_DIM + 1]

    for dil, on_ref, ls_ref in ((16, o16_ref, l16_ref), (4, o4_ref, l4_ref), (1, None, None)):
        nb = t // (dil * BAND)
        shift = nb.bit_length() - 1
        bias_a = neg - (slope_a * float(dil)) * distf
        bias_b = neg - (slope_b * float(dil)) * distf

        def body(it, carry, dil=dil, nb=nb, shift=shift, bias_a=bias_a, bias_b=bias_b,
                 on_ref=on_ref, ls_ref=ls_ref):
            blocks = [it * group + g for g in range(group)]
            res = [i >> shift for i in blocks]
            blk = [i & (nb - 1) for i in blocks]
            starts = [dil * BAND * b + r for b, r in zip(blk, res)]
            prevs = [dil * BAND * jnp.maximum(b - 1, 0) + r for b, r in zip(blk, res)]
            outs = _attn_blocks(q_ref, k_ref, v_ref, starts, prevs, [b == 0 for b in blk], dil,
                                bias_a, bias_b, first_mask, is_a)
            for start, (o, lse) in zip(starts, outs):
                if dil > 1:
                    on_ref[pl.ds(start, BAND, stride=dil), :] = o
                    ls_ref[pl.ds(start, BAND, stride=dil), :] = lse
                else:
                    rows = pl.ds(start, BAND)
                    l16 = l16_ref[rows, :]
                    l4 = l4_ref[rows, :]
                    mx = jnp.maximum(jnp.maximum(lse, l16), l4)
                    w1 = jnp.exp(lse - mx)
                    w16 = jnp.exp(l16 - mx)
                    w4 = jnp.exp(l4 - mx)
                    num = w1 * o + w16 * o16_ref[rows, :] + w4 * o4_ref[rows, :]
                    o_ref[rows, :] = num * (1.0 / (w1 + w16 + w4))
            return carry

        lax.fori_loop(0, dil * nb // group, body, 0)


def _attn_prompt(at, slopes, n, t, group=4):
    assert (t // BAND) % group == 0
    kern = functools.partial(_attn_kernel, t=t, group=group)
    pairs = AT_HEADS // 2
    return pl.pallas_call(
        kern,
        grid=(n, pairs),
        in_specs=[
            pl.BlockSpec((t, LANES), lambda b, p: (b, p)),
            pl.BlockSpec((t, LANES), lambda b, p: (b, pairs + p)),
            pl.BlockSpec((t, LANES), lambda b, p: (b, 2 * pairs + p)),
            pl.BlockSpec((1, SUBLANES, LANES), lambda b, p: (p, 0, 0)),
        ],
        out_specs=pl.BlockSpec((t, LANES), lambda b, p: (b, p)),
        out_shape=jax.ShapeDtypeStruct((n * t, AT_WIDTH), F32),
        scratch_shapes=[pltpu.VMEM((t, LANES), F32) for _ in range(4)],
        compiler_params=pltpu.CompilerParams(
            dimension_semantics=("arbitrary", "arbitrary"), vmem_limit_bytes=VMEM_LIMIT),
        name="attn_prompt",
    )(at, at, at, slopes)


def _attn_sample_kernel(ck_ref, cv_ref, nxk_ref, nxv_ref, q_ref, kn_ref, vn_ref, slope_ref,
                        o_ref, nk_ref, nv_ref, ks_ref, vs_ref, *, tt, n_tiles):
    j = pl.program_id(1)
    far = tt // DILATIONS[-1]

    nk_ref[0, 0, 0:tt - 1] = ck_ref[0, 0, 1:tt]
    nv_ref[0, 0, 0:tt - 1] = cv_ref[0, 0, 1:tt]

    @pl.when(j < n_tiles - 1)
    def _():
        nk_ref[0, 0, tt - 1] = nxk_ref[0, 0, 0]
        nv_ref[0, 0, tt - 1] = nxv_ref[0, 0, 0]
        for i in range(far):
            ks_ref[j * far + i] = ck_ref[0, 0, i * DILATIONS[-1]]
            vs_ref[j * far + i] = cv_ref[0, 0, i * DILATIONS[-1]]

    @pl.when(j == n_tiles - 1)
    def _():
        q = q_ref[0] * (AT_HEAD_DIM ** -0.5)
        k_new = kn_ref[0]
        v_new = vn_ref[0]
        nk_ref[0, 0, tt - 1] = k_new
        nv_ref[0, 0, tt - 1] = v_new
        slope = slope_ref[:, 0:1]
        n_far = (n_tiles - 1) * far

        def scores(k3, d):
            s = jnp.sum(k3 * q[None], axis=-1, keepdims=True)
            return s - slope[None] * d.astype(F32)

        idx = lax.broadcasted_iota(jnp.int32, (n_far, AT_HEADS, 1), 0)
        shift = far.bit_length() - 1
        d_far = n_tiles * tt - ((idx >> shift) * tt + (idx & (far - 1)) * DILATIONS[-1])
        s_far = scores(ks_ref[...], d_far)
        d_near = tt - lax.broadcasted_iota(jnp.int32, (tt, AT_HEADS, 1), 0)
        cnt = jnp.zeros((tt, AT_HEADS, 1), F32)
        for dil in DILATIONS:
            hit = jnp.logical_and((d_near & (dil - 1)) == 0, d_near <= BAND * dil)
            cnt = cnt + hit.astype(F32)
        s_near = jnp.where(cnt > 0.0, scores(ck_ref[0, 0], d_near), -jnp.inf)
        s_new = jnp.sum(k_new * q, axis=-1, keepdims=True)

        mx = jnp.maximum(jnp.maximum(jnp.max(s_far, axis=0), jnp.max(s_near, axis=0)), s_new)
        p_far = jnp.exp(s_far - mx[None])
        p_near = cnt * jnp.exp(s_near - mx[None])
        p_new = float(len(DILATIONS)) * jnp.exp(s_new - mx)
        l = jnp.sum(p_far, axis=0) + jnp.sum(p_near, axis=0) + p_new
        acc = (jnp.sum(p_far * vs_ref[...], axis=0) + jnp.sum(p_near * cv_ref[0, 0], axis=0)
               + p_new * v_new)
        o_ref[0] = acc * (1.0 / l)


def _attn_sample(ck, cv, q, k_new, v_new, slopes, tt):
    _, n, length, _, _ = ck.shape
    n_tiles = length // tt
    assert length % tt == 0 and tt % DILATIONS[-1] == 0 and tt >= BAND * DILATIONS[-2]
    assert length == BAND * DILATIONS[-1]
    far = tt // DILATIONS[-1]
    assert far & (far - 1) == 0
    tile = pl.BlockSpec((1, 1, tt, AT_HEADS, AT_HEAD_DIM), lambda b, j: (0, b, j, 0, 0))
    nxt = pl.BlockSpec((1, 1, 1, AT_HEADS, AT_HEAD_DIM),
                       lambda b, j: (0, b, jnp.minimum((j + 1) * tt, length - 1), 0, 0))
    row = pl.BlockSpec((1, AT_HEADS, AT_HEAD_DIM), lambda b, j: (b, 0, 0))
    return pl.pallas_call(
        functools.partial(_attn_sample_kernel, tt=tt, n_tiles=n_tiles),
        grid=(n, n_tiles),
        in_specs=[tile, tile, nxt, nxt, row, row, row,
                  pl.BlockSpec((AT_HEADS, LANES), lambda b, j: (0, 0))],
        out_specs=[row, tile, tile],
        out_shape=[
            jax.ShapeDtypeStruct((n, AT_HEADS, AT_HEAD_DIM), F32),
            jax.ShapeDtypeStruct(ck.shape, F32),
            jax.ShapeDtypeStruct(cv.shape, F32),
        ],
        scratch_shapes=[pltpu.VMEM(((n_tiles - 1) * far, AT_HEADS, AT_HEAD_DIM), F32)
                        for _ in range(2)],
        compiler_params=pltpu.CompilerParams(
            dimension_semantics=("arbitrary", "arbitrary"), vmem_limit_bytes=VMEM_LIMIT),
        name="attn_sample",
    )(ck, cv, ck, cv, q, k_new, v_new, slopes)


def _outffn_kernel(x_ref, od_ref, oa_ref, wod_ref, woa_ref, ln2_ref, wup_ref, wdown_ref, lnf_ref,
                   y_ref, *, ff_chunk):
    mix = (jnp.dot(od_ref[...].astype(BF16), wod_ref[...], preferred_element_type=F32)
           + jnp.dot(oa_ref[...].astype(BF16), woa_ref[...], preferred_element_type=F32))
    x1 = x_ref[...] + mix
    h = _rmsnorm(x1, ln2_ref[...]).astype(BF16)
    acc = jnp.zeros_like(x1)
    for c in range(D_FF // ff_chunk):
        cols = slice(c * ff_chunk, (c + 1) * ff_chunk)
        up = jnp.dot(h, wup_ref[:, cols], preferred_element_type=F32)
        act = jnp.square(jnp.maximum(up, 0.0)).astype(BF16)
        acc = acc + jnp.dot(act, wdown_ref[cols, :], preferred_element_type=F32)
    y_ref[...] = _rmsnorm(x1 + acc, lnf_ref[...])


def _outffn(x, od, oa, wod, woa, ln2, wup, wdown, lnf, tm):
    m = x.shape[0]
    row = lambda i: (i, 0)
    const = lambda i: (0, 0)
    resident = functools.partial(pl.BlockSpec, index_map=const, pipeline_mode=pl.Buffered(1))
    return pl.pallas_call(
        functools.partial(_outffn_kernel, ff_chunk=1024),
        grid=(m // tm,),
        in_specs=[
            pl.BlockSpec((tm, D_MODEL), row),
            pl.BlockSpec((tm, DN_V), row),
            pl.BlockSpec((tm, AT_WIDTH), row),
            resident((DN_V, D_MODEL)),
            resident((AT_WIDTH, D_MODEL)),
            resident((1, D_MODEL)),
            resident((D_MODEL, D_FF)),
            resident((D_FF, D_MODEL)),
            resident((1, D_MODEL)),
        ],
        out_specs=pl.BlockSpec((tm, D_MODEL), row),
        out_shape=jax.ShapeDtypeStruct((m, D_MODEL), F32),
        compiler_params=pltpu.CompilerParams(
            dimension_semantics=("arbitrary",), vmem_limit_bytes=VMEM_LIMIT),
        name="outffn",
    )(x, od, oa, wod, woa, ln2, wup, wdown, lnf)


def _layer_params(ln_mix, w_in, dn_conv_w, dn_a_log, dn_dt_bias, dn_norm, w_out, ln_ffn, w_ffn_up,
                  w_ffn_down, ln_final):
    w = w_in[0]
    o_z = CONV_CH
    o_b = o_z + DN_V
    o_at = o_b + 2 * DN_HEADS
    wg = jnp.zeros((D_MODEL, LANES), F32).at[:, :2 * DN_HEADS].set(w[:, o_b:o_at])
    par = jnp.zeros((SUBLANES, LANES), F32)
    par = par.at[0, GATE_A0:GATE_A0 + DN_HEADS].set(dn_a_log[0].astype(F32))
    par = par.at[1, GATE_A0:GATE_A0 + DN_HEADS].set(dn_dt_bias[0].astype(F32))
    cw = jnp.zeros((SUBLANES, CONV_CH), F32).at[:CONV_W].set(dn_conv_w[0])
    slopes = 2.0 ** (-8.0 * jnp.arange(1, AT_HEADS + 1, dtype=F32) / AT_HEADS)
    pair_slopes = jnp.broadcast_to(
        jnp.repeat(slopes.reshape(AT_HEADS // 2, 2), AT_HEAD_DIM, axis=1)[:, None, :],
        (AT_HEADS // 2, SUBLANES, LANES))
    head_slopes = jnp.broadcast_to(slopes[:, None], (AT_HEADS, LANES))
    return dict(
        ln_mix=ln_mix[0][None, :],
        wdn=w[:, :o_z].astype(BF16), wz=w[:, o_z:o_b].astype(BF16), wat=w[:, o_at:].astype(BF16),
        wg=wg.astype(BF16), cw=cw, par=par, nrm=dn_norm[0][None, :].astype(F32),
        pair_slopes=pair_slopes, head_slopes=head_slopes,
        wod=w_out[0][:DN_V].astype(BF16), woa=w_out[0][DN_V:].astype(BF16),
        ln_ffn=ln_ffn[0][None, :], wup=w_ffn_up[0].astype(BF16), wdown=w_ffn_down[0].astype(BF16),
        ln_final=ln_final[None, :])


def _prompt_group(x, p):
    n, t, _ = x.shape
    assert t % (DILATIONS[-1] * BAND) == 0
    x2 = x.reshape(n * t, D_MODEL)
    y, zs, gb, at, tail = _inproj_prompt(x2, p, n, t, tm=512)
    od, ssm = _gdn_prompt(y, zs, gb, p["nrm"], n, t, nc=4)
    oa = _attn_prompt(at, p["pair_slopes"], n, t)
    out = _outffn(x2, od, oa, p["wod"], p["woa"], p["ln_ffn"], p["wup"], p["wdown"], p["ln_final"],
                  tm=512)
    keep = min(WINDOW, t)
    at3 = at.reshape(n, t, 3 * AT_WIDTH)
    win_k = at3[:, t - keep:, AT_WIDTH:2 * AT_WIDTH].reshape(n, keep, AT_HEADS, AT_HEAD_DIM)
    win_v = at3[:, t - keep:, 2 * AT_WIDTH:].reshape(n, keep, AT_HEADS, AT_HEAD_DIM)
    return (out.reshape(n, t, D_MODEL), tail[:, HALO - (CONV_W - 1):][None], ssm[None],
            win_k[None], win_v[None])


def _sample_group(x, conv_st, ssm_st, k_st, v_st, p):
    n, t, _ = x.shape
    assert t == 1 and k_st.shape[2] == WINDOW
    x2 = x.reshape(n, D_MODEL)
    hist = jnp.swapaxes(conv_st[0].astype(F32), 0, 1)
    y, zs, gb, at, tail = _inproj_sample(x2, hist, p)
    od, ssm = _gdn_step(y, zs, gb, ssm_st[0].astype(F32), p["nrm"], nb=8)
    heads = lambda a: a.reshape(n, AT_HEADS, AT_HEAD_DIM)
    oa, win_k, win_v = _attn_sample(
        k_st.astype(F32), v_st.astype(F32), heads(at[:, :AT_WIDTH]),
        heads(at[:, AT_WIDTH:2 * AT_WIDTH]), heads(at[:, 2 * AT_WIDTH:]), p["head_slopes"], tt=512)
    out = _outffn(x2, od, oa.reshape(n, AT_WIDTH), p["wod"], p["woa"], p["ln_ffn"], p["wup"],
                  p["wdown"], p["ln_final"], tm=n)
    return (out.reshape(n, 1, D_MODEL), jnp.swapaxes(tail, 0, 1)[None], ssm[None], win_k, win_v)


def kernel(x_prompt, x_sample, state_conv, state_ssm, cache_win_k, cache_win_v, ln_mix, w_in,
           dn_conv_w, dn_a_log, dn_dt_bias, dn_norm, w_out, ln_ffn, w_ffn_up, w_ffn_down, ln_final):
    p = _layer_params(ln_mix, w_in, dn_conv_w, dn_a_log, dn_dt_bias, dn_norm, w_out, ln_ffn,
                      w_ffn_up, w_ffn_down, ln_final)
    y_p, p_conv, p_ssm, p_k, p_v = _prompt_group(x_prompt, p)
    y_s, s_conv, s_ssm, s_k, s_v = _sample_group(x_sample, state_conv, state_ssm, cache_win_k,
                                                 cache_win_v, p)
    return (y_p, y_s, p_conv, p_ssm, p_k, p_v, s_conv, s_ssm, s_k, s_v)
```

```python
import functools

import jax
import jax.numpy as jnp
from jax import lax
from jax.experimental import pallas as pl
from jax.experimental.pallas import tpu as pltpu

F32 = jnp.float32
BF16 = jnp.bfloat16

D_MODEL = 1024
DN_HEADS = 4
DN_DK = 128
DN_DV = 128
DN_QK = DN_HEADS * DN_DK
DN_V = DN_HEADS * DN_DV
CONV_CH = 2 * DN_QK + DN_V
CONV_W = 4
CHUNK = 64
AT_HEADS = 8
AT_HEAD_DIM = 64
AT_WIDTH = AT_HEADS * AT_HEAD_DIM
BAND = 128
DILATIONS = (1, 4, 16)
WINDOW = 2048
D_FF = 4 * D_MODEL
EPS = 1e-6

LANES = 128
SUBLANES = 8
HALO = SUBLANES
GATE_B0 = 0
GATE_A0 = DN_HEADS
VMEM_LIMIT = 56 * 1024 * 1024


def _rmsnorm(x, w):
    return x * lax.rsqrt(jnp.mean(x * x, axis=-1, keepdims=True) + EPS) * w


def _sigmoid(x):
    return 1.0 / (1.0 + jnp.exp(-x))


def _softplus(x):
    return jnp.maximum(x, 0.0) + jnp.log(1.0 + jnp.exp(-jnp.abs(x)))


def _mm(a, b):
    return jnp.dot(a.astype(BF16), b.astype(BF16), preferred_element_type=F32)


def _mm_nt(a, b):
    return lax.dot_general(a.astype(BF16), b.astype(BF16), (((1,), (1,)), ((), ())),
                           preferred_element_type=F32)


def _mm_tn(a, b):
    return lax.dot_general(a.astype(BF16), b.astype(BF16), (((0,), (0,)), ((), ())),
                           preferred_element_type=F32)


def _mm_exact(a, b):
    return jnp.dot(a, b, precision=lax.Precision.HIGHEST, preferred_element_type=F32)


def _store_mixer_inputs(conv, y_ref):
    y = conv * _sigmoid(conv)
    for blk in range(2 * DN_HEADS):
        cols = slice(blk * DN_DK, (blk + 1) * DN_DK)
        part = y[:, cols]
        part = part * lax.rsqrt(jnp.sum(part * part, axis=-1, keepdims=True) + EPS)
        if blk < DN_HEADS:
            part = part * (DN_DK ** -0.5)
        y_ref[:, cols] = part
    y_ref[:, 2 * DN_QK:] = y[:, 2 * DN_QK:]


def _store_gates(gate, par_ref, gb_ref):
    lane = lax.broadcasted_iota(jnp.int32, (1, LANES), 1)
    beta = _sigmoid(gate)
    g = -jnp.exp(par_ref[0:1, :]) * _softplus(gate + par_ref[1:2, :])
    gb_ref[...] = jnp.where(lane < GATE_A0, beta, g)


def _inproj_prompt_kernel(x_ref, ln_ref, wdn_ref, wz_ref, wat_ref, wg_ref, cw_ref, par_ref,
                          y_ref, zs_ref, gb_ref, at_ref, tail_ref, xc_ref, *, tm, seq_tiles):
    j = lax.rem(pl.program_id(0), seq_tiles)

    @pl.when(pl.program_id(0) == 0)
    def _():
        xc_ref[tm:tm + HALO, :] = jnp.zeros((HALO, CONV_CH), F32)

    h = _rmsnorm(x_ref[...], ln_ref[...]).astype(BF16)

    xc_ref[0:HALO, :] = jnp.where(j == 0, 0.0, xc_ref[tm:tm + HALO, :])
    xc_ref[HALO:HALO + tm, :] = jnp.dot(h, wdn_ref[...], preferred_element_type=F32)
    tail_ref[0] = xc_ref[tm:tm + HALO, :]

    lo = HALO - (CONV_W - 1)
    conv = xc_ref[lo:lo + tm, :] * cw_ref[0:1, :]
    for i in range(1, CONV_W):
        conv = conv + xc_ref[lo + i:lo + i + tm, :] * cw_ref[i:i + 1, :]
    _store_mixer_inputs(conv, y_ref)

    at_ref[...] = jnp.dot(h, wat_ref[...], preferred_element_type=F32)
    z = jnp.dot(h, wz_ref[...], preferred_element_type=F32)
    zs_ref[...] = z * _sigmoid(z)
    _store_gates(jnp.dot(h, wg_ref[...], preferred_element_type=F32), par_ref, gb_ref)


def _inproj_sample_kernel(x_ref, hist_ref, ln_ref, wdn_ref, wz_ref, wat_ref, wg_ref, cw_ref, par_ref,
                          y_ref, zs_ref, gb_ref, at_ref, tail_ref):
    h = _rmsnorm(x_ref[...], ln_ref[...]).astype(BF16)
    dn = jnp.dot(h, wdn_ref[...], preferred_element_type=F32)
    z = jnp.dot(h, wz_ref[...], preferred_element_type=F32)
    zs_ref[...] = z * _sigmoid(z)
    at_ref[...] = jnp.dot(h, wat_ref[...], preferred_element_type=F32)
    gate = jnp.dot(h, wg_ref[...], preferred_element_type=F32)

    conv = hist_ref[0] * cw_ref[0:1, :]
    for i in range(1, CONV_W - 1):
        conv = conv + hist_ref[i] * cw_ref[i:i + 1, :]
    conv = conv + dn * cw_ref[CONV_W - 1:CONV_W, :]
    for i in range(1, CONV_W - 1):
        tail_ref[i - 1] = hist_ref[i]
    tail_ref[CONV_W - 2] = dn
    _store_mixer_inputs(conv, y_ref)
    _store_gates(gate, par_ref, gb_ref)


def _weight_specs():
    const = lambda i: (0, 0)
    return [
        pl.BlockSpec((1, D_MODEL), const),
        pl.BlockSpec((D_MODEL, CONV_CH), const),
        pl.BlockSpec((D_MODEL, DN_V), const),
        pl.BlockSpec((D_MODEL, 3 * AT_WIDTH), const),
        pl.BlockSpec((D_MODEL, LANES), const),
        pl.BlockSpec((SUBLANES, CONV_CH), const),
        pl.BlockSpec((SUBLANES, LANES), const),
    ]


def _inproj_prompt(x, p, n, t, tm):
    m = n * t
    seq_tiles = t // tm
    row = lambda i: (i, 0)
    return pl.pallas_call(
        functools.partial(_inproj_prompt_kernel, tm=tm, seq_tiles=seq_tiles),
        grid=(m // tm,),
        in_specs=[pl.BlockSpec((tm, D_MODEL), row)] + _weight_specs(),
        out_specs=[
            pl.BlockSpec((tm, CONV_CH), row),
            pl.BlockSpec((tm, DN_V), row),
            pl.BlockSpec((tm, LANES), row),
            pl.BlockSpec((tm, 3 * AT_WIDTH), row),
            pl.BlockSpec((1, HALO, CONV_CH), lambda i: (i // seq_tiles, 0, 0)),
        ],
        out_shape=[
            jax.ShapeDtypeStruct((m, CONV_CH), F32),
            jax.ShapeDtypeStruct((m, DN_V), F32),
            jax.ShapeDtypeStruct((m, LANES), F32),
            jax.ShapeDtypeStruct((m, 3 * AT_WIDTH), F32),
            jax.ShapeDtypeStruct((n, HALO, CONV_CH), F32),
        ],
        scratch_shapes=[pltpu.VMEM((HALO + tm, CONV_CH), F32)],
        compiler_params=pltpu.CompilerParams(
            dimension_semantics=("arbitrary",), vmem_limit_bytes=VMEM_LIMIT),
        name="inproj_prompt",
    )(x, p["ln_mix"], p["wdn"], p["wz"], p["wat"], p["wg"], p["cw"], p["par"])


def _inproj_sample(x, hist, p):
    n = x.shape[0]
    full = lambda i: (0, 0)
    full3 = lambda i: (0, 0, 0)
    return pl.pallas_call(
        _inproj_sample_kernel,
        grid=(1,),
        in_specs=[pl.BlockSpec((n, D_MODEL), full),
                  pl.BlockSpec((CONV_W - 1, n, CONV_CH), full3)] + _weight_specs(),
        out_specs=[
            pl.BlockSpec((n, CONV_CH), full),
            pl.BlockSpec((n, DN_V), full),
            pl.BlockSpec((n, LANES), full),
            pl.BlockSpec((n, 3 * AT_WIDTH), full),
            pl.BlockSpec((CONV_W - 1, n, CONV_CH), full3),
        ],
        out_shape=[
            jax.ShapeDtypeStruct((n, CONV_CH), F32),
            jax.ShapeDtypeStruct((n, DN_V), F32),
            jax.ShapeDtypeStruct((n, LANES), F32),
            jax.ShapeDtypeStruct((n, 3 * AT_WIDTH), F32),
            jax.ShapeDtypeStruct((CONV_W - 1, n, CONV_CH), F32),
        ],
        compiler_params=pltpu.CompilerParams(
            dimension_semantics=("arbitrary",), vmem_limit_bytes=VMEM_LIMIT),
        name="inproj_sample",
    )(x, hist, p["ln_mix"], p["wdn"], p["wz"], p["wat"], p["wg"], p["cw"], p["par"])


def _unit_lower_inverses(ms, eye, same16, same32):
    n = [jnp.where(same16, m, 0.0) for m in ms]
    p = [eye - a for a in n]
    n = [_mm(a, a) for a in n]
    for last in (False, False, True):
        pn = [_mm(a, b) for a, b in zip(p, n)]
        if not last:
            n = [_mm(a, a) for a in n]
        p = [a + b for a, b in zip(p, pn)]
    off16 = jnp.logical_and(same32, jnp.logical_not(same16))
    pe = [_mm(a, jnp.where(off16, m, 0.0)) for a, m in zip(p, ms)]
    t = [a - _mm(b, a) for a, b in zip(p, pe)]
    te = [_mm(a, jnp.where(same32, 0.0, m)) for a, m in zip(t, ms)]
    return [a - _mm(b, a) for a, b in zip(t, te)]


def _gated_norm(o, zs, nrm):
    return o * lax.rsqrt(jnp.mean(o * o, axis=-1, keepdims=True) + EPS) * nrm * zs


def _gdn_kernel(y_ref, zs_ref, gb_ref, nrm_ref, od_ref, s_ref, *, nc):
    tc = nc * CHUNK

    @pl.when(pl.program_id(1) == 0)
    def _():
        s_ref[...] = jnp.zeros(s_ref.shape, F32)

    y = y_ref[...]
    zs = zs_ref[...]
    gb = gb_ref[...]

    ri = lax.broadcasted_iota(jnp.int32, (tc, tc), 0)
    ci = lax.broadcasted_iota(jnp.int32, (tc, tc), 1)
    lblk = jnp.logical_and((ri >> 6) == (ci >> 6), ci <= ri).astype(F32)
    gc_all = _mm_exact(lblk, gb)
    gc_t = gc_all.T

    r64 = lax.broadcasted_iota(jnp.int32, (CHUNK, CHUNK), 0)
    c64 = lax.broadcasted_iota(jnp.int32, (CHUNK, CHUNK), 1)
    incl = c64 <= r64
    strict = c64 < r64
    eye = (c64 == r64).astype(F32)
    same16 = (r64 >> 4) == (c64 >> 4)
    same32 = (r64 >> 5) == (c64 >> 5)
    nrm = nrm_ref[...]

    cells = [(c, h) for c in range(nc) for h in range(DN_HEADS)]
    qn, kn, kb, vb, gcc, egc, gl, decay = [], [], [], [], [], [], [], []
    for c, h in cells:
        rs = slice(c * CHUNK, (c + 1) * CHUNK)
        beta = gb[rs, GATE_B0 + h:GATE_B0 + h + 1]
        col = gc_all[rs, GATE_A0 + h:GATE_A0 + h + 1]
        row = gc_t[GATE_A0 + h:GATE_A0 + h + 1, rs]
        qn.append(y[rs, h * DN_DK:(h + 1) * DN_DK])
        kn.append(y[rs, DN_QK + h * DN_DK:DN_QK + (h + 1) * DN_DK])
        kb.append(kn[-1] * beta)
        vb.append(y[rs, 2 * DN_QK + h * DN_DV:2 * DN_QK + (h + 1) * DN_DV] * beta)
        gcc.append(col)
        egc.append(jnp.exp(col))
        gl.append(gc_all[(c + 1) * CHUNK - 1:(c + 1) * CHUNK, GATE_A0 + h:GATE_A0 + h + 1])
        decay.append(jnp.exp(jnp.minimum(col - row, 0.0)))

    kk = [_mm_nt(a, b) for a, b in zip(kb, kn)]
    qk = [_mm_nt(a, b) for a, b in zip(qn, kn)]
    ms = [jnp.where(strict, a * d, 0.0) for a, d in zip(kk, decay)]
    qk = [jnp.where(incl, a * d, 0.0) for a, d in zip(qk, decay)]
    ts = _unit_lower_inverses(ms, eye, same16, same32)
    sol = [_mm(t, jnp.concatenate([a, b * e], axis=1))
           for t, a, b, e in zip(ts, vb, kb, egc)]

    s = [s_ref[0, h] for h in range(DN_HEADS)]
    for c in range(nc):
        ids = [c * DN_HEADS + h for h in range(DN_HEADS)]
        ws = [_mm(sol[i][:, DN_DV:], s[h]) for h, i in enumerate(ids)]
        qs = [_mm(qn[i] * egc[i], s[h]) for h, i in enumerate(ids)]
        v_new = [sol[i][:, :DN_DV] - a for i, a in zip(ids, ws)]
        intra = [_mm(qk[i], v) for i, v in zip(ids, v_new)]
        upd = [_mm_tn(kn[i] * jnp.exp(gl[i] - gcc[i]), v) for i, v in zip(ids, v_new)]
        s = [s[h] * jnp.exp(gl[i]) + upd[h] for h, i in enumerate(ids)]
        rs = slice(c * CHUNK, (c + 1) * CHUNK)
        for h in range(DN_HEADS):
            cols = slice(h * DN_DV, (h + 1) * DN_DV)
            od_ref[rs, cols] = _gated_norm(qs[h] + intra[h], zs[rs, cols], nrm)
    for h in range(DN_HEADS):
        s_ref[0, h] = s[h]


def _gdn_prompt(y, zs, gb, nrm, n, t, nc):
    tc = nc * CHUNK
    n_tiles = t // tc
    tile = lambda b, j: (b * n_tiles + j, 0)
    state = pl.BlockSpec((1, DN_HEADS, DN_DK, DN_DV), lambda b, j: (b, 0, 0, 0))
    return pl.pallas_call(
        functools.partial(_gdn_kernel, nc=nc),
        grid=(n, n_tiles),
        in_specs=[
            pl.BlockSpec((tc, CONV_CH), tile),
            pl.BlockSpec((tc, DN_V), tile),
            pl.BlockSpec((tc, LANES), tile),
            pl.BlockSpec((1, DN_DV), lambda b, j: (0, 0)),
        ],
        out_specs=[pl.BlockSpec((tc, DN_V), tile), state],
        out_shape=[
            jax.ShapeDtypeStruct((n * t, DN_V), F32),
            jax.ShapeDtypeStruct((n, DN_HEADS, DN_DK, DN_DV), F32),
        ],
        compiler_params=pltpu.CompilerParams(
            dimension_semantics=("arbitrary", "arbitrary"), vmem_limit_bytes=VMEM_LIMIT),
        name="gdn_prompt",
    )(y, zs, gb, nrm)


def _gdn_step_kernel(y_ref, zs_ref, gb_ref, s0_ref, nrm_ref, od_ref, s_ref, *, nb):
    y = y_ref[...]
    zs = zs_ref[...]
    gb = gb_ref[...]
    nrm = nrm_ref[...]
    rowid = lax.broadcasted_iota(jnp.int32, (nb, DN_DK), 0)
    cells = [(b, h) for b in range(nb) for h in range(DN_HEADS)]
    qh = [y[:, h * DN_DK:(h + 1) * DN_DK] for h in range(DN_HEADS)]
    kh = [y[:, DN_QK + h * DN_DK:DN_QK + (h + 1) * DN_DK] for h in range(DN_HEADS)]
    vh = [y[:, 2 * DN_QK + h * DN_DV:2 * DN_QK + (h + 1) * DN_DV] for h in range(DN_HEADS)]
    ks = [_mm(kh[h], s0_ref[b, h])[b:b + 1] for b, h in cells]
    qs = [_mm(qh[h], s0_ref[b, h])[b:b + 1] for b, h in cells]
    rs, egs = [], []
    for i, (b, h) in enumerate(cells):
        beta = gb[b:b + 1, GATE_B0 + h:GATE_B0 + h + 1]
        eg = jnp.exp(gb[b:b + 1, GATE_A0 + h:GATE_A0 + h + 1])
        rs.append(beta * (vh[h][b:b + 1] - eg * ks[i]))
        egs.append(eg)
    upd = [_mm_tn(jnp.where(rowid == b, kh[h], 0.0), jnp.broadcast_to(rs[i], (nb, DN_DV)))
           for i, (b, h) in enumerate(cells)]
    for i, (b, h) in enumerate(cells):
        s_ref[b, h] = s0_ref[b, h] * egs[i] + upd[i]
        qk = jnp.sum(qh[h][b:b + 1] * kh[h][b:b + 1], axis=-1, keepdims=True)
        cols = slice(h * DN_DV, (h + 1) * DN_DV)
        od_ref[b:b + 1, cols] = _gated_norm(egs[i] * qs[i] + qk * rs[i], zs[b:b + 1, cols], nrm)


def _gdn_step(y, zs, gb, s0, nrm, nb):
    n = y.shape[0]
    assert n % nb == 0
    row = lambda i: (i, 0)
    state = pl.BlockSpec((nb, DN_HEADS, DN_DK, DN_DV), lambda i: (i, 0, 0, 0))
    return pl.pallas_call(
        functools.partial(_gdn_step_kernel, nb=nb),
        grid=(n // nb,),
        in_specs=[
            pl.BlockSpec((nb, CONV_CH), row),
            pl.BlockSpec((nb, DN_V), row),
            pl.BlockSpec((nb, LANES), row),
            state,
            pl.BlockSpec((1, DN_DV), lambda i: (0, 0)),
        ],
        out_specs=[pl.BlockSpec((nb, DN_V), row), state],
        out_shape=[
            jax.ShapeDtypeStruct((n, DN_V), F32),
            jax.ShapeDtypeStruct((n, DN_HEADS, DN_DK, DN_DV), F32),
        ],
        compiler_params=pltpu.CompilerParams(
            dimension_semantics=("arbitrary",), vmem_limit_bytes=VMEM_LIMIT),
        name="gdn_step",
    )(y, zs, gb, s0, nrm)


def _attn_blocks(q_ref, k_ref, v_ref, starts, prevs, firsts, dil, bias_a, bias_b, first_mask, is_a):
    def rows(ref, at):
        if dil == 1:
            return ref[pl.ds(at, BAND), :]
        return ref[pl.ds(at, BAND, stride=dil), :]

    qs = [rows(q_ref, a) * (AT_HEAD_DIM ** -0.5) for a in starts]
    k2 = [jnp.concatenate([rows(k_ref, p), rows(k_ref, a)], axis=0).astype(BF16)
          for a, p in zip(starts, prevs)]
    v2 = [jnp.concatenate([rows(v_ref, p), rows(v_ref, a)], axis=0).astype(BF16)
          for a, p in zip(starts, prevs)]
    edges = [jnp.where(f, first_mask, 0.0) for f in firsts]

    heads = [(i, a) for i in range(len(starts)) for a in (True, False)]
    s = [_mm_nt(jnp.where(is_a, qs[i], 0.0) if a else jnp.where(is_a, 0.0, qs[i]), k2[i])
         for i, a in heads]
    s = [x + (bias_a if a else bias_b) + edges[i] for x, (i, a) in zip(s, heads)]
    mx = [jnp.max(x, axis=-1, keepdims=True) for x in s]
    p = [jnp.exp(x - m) for x, m in zip(s, mx)]
    l = [jnp.sum(x, axis=-1, keepdims=True) for x in p]
    pv = [jnp.dot(x.astype(BF16), v2[i], preferred_element_type=F32)
          for x, (i, a) in zip(p, heads)]
    outs = []
    for i in range(len(starts)):
        a, b = 2 * i, 2 * i + 1
        o = jnp.where(is_a, pv[a] * (1.0 / l[a]), pv[b] * (1.0 / l[b]))
        lse = jnp.where(is_a, mx[a] + jnp.log(l[a]), mx[b] + jnp.log(l[b]))
        outs.append((o, lse))
    return outs


def _attn_kernel(q_ref, k_ref, v_ref, slope_ref, o_ref, o16_ref, l16_ref, o4_ref, l4_ref, *, t,
                 group):
    is_a = lax.broadcasted_iota(jnp.int32, (BAND, LANES), 1) < AT_HEAD_DIM
    ri = lax.broadcasted_iota(jnp.int32, (BAND, 2 * BAND), 0)
    cj = lax.broadcasted_iota(jnp.int32, (BAND, 2 * BAND), 1)
    dist = ri + BAND - cj
    distf = dist.astype(F32)
    neg = jnp.where(jnp.logical_and(dist >= 0, dist <= BAND), 0.0, -jnp.inf)
    first_mask = jnp.where(cj < BAND, -jnp.inf, 0.0)
    slope_a = slope_ref[0, 0:1, 0:1]
    slope_b = slope_ref[0, 0:1, AT_HEAD_DIM:AT_HEAD_DIM + 1]

    for dil, on_ref, ls_ref in ((16, o16_ref, l16_ref), (4, o4_ref, l4_ref), (1, None, None)):
        nb = t // (dil * BAND)
        shift = nb.bit_length() - 1
        bias_a = neg - (slope_a * float(dil)) * distf
        bias_b = neg - (slope_b * float(dil)) * distf

        def body(it, carry, dil=dil, nb=nb, shift=shift, bias_a=bias_a, bias_b=bias_b,
                 on_ref=on_ref, ls_ref=ls_ref):
            blocks = [it * group + g for g in range(group)]
            res = [i >> shift for i in blocks]
            blk = [i & (nb - 1) for i in blocks]
            starts = [dil * BAND * b + r for b, r in zip(blk, res)]
            prevs = [dil * BAND * jnp.maximum(b - 1, 0) + r for b, r in zip(blk, res)]
            outs = _attn_blocks(q_ref, k_ref, v_ref, starts, prevs, [b == 0 for b in blk], dil,
                                bias_a, bias_b, first_mask, is_a)
            for start, (o, lse) in zip(starts, outs):
                if dil > 1:
                    on_ref[pl.ds(start, BAND, stride=dil), :] = o
                    ls_ref[pl.ds(start, BAND, stride=dil), :] = lse
                else:
                    rows = pl.ds(start, BAND)
                    l16 = l16_ref[rows, :]
                    l4 = l4_ref[rows, :]
                    mx = jnp.maximum(jnp.maximum(lse, l16), l4)
                    w1 = jnp.exp(lse - mx)
                    w16 = jnp.exp(l16 - mx)
                    w4 = jnp.exp(l4 - mx)
                    num = w1 * o + w16 * o16_ref[rows, :] + w4 * o4_ref[rows, :]
                    o_ref[rows, :] = num * (1.0 / (w1 + w16 + w4))
            return carry

        lax.fori_loop(0, dil * nb // group, body, 0)


def _attn_prompt(at, slopes, n, t, group=4):
    assert (t // BAND) % group == 0
    kern = functools.partial(_attn_kernel, t=t, group=group)
    pairs = AT_HEADS // 2
    return pl.pallas_call(
        kern,
        grid=(n, pairs),
        in_specs=[
            pl.BlockSpec((t, LANES), lambda b, p: (b, p)),
            pl.BlockSpec((t, LANES), lambda b, p: (b, pairs + p)),
            pl.BlockSpec((t, LANES), lambda b, p: (b, 2 * pairs + p)),
            pl.BlockSpec((1, SUBLANES, LANES), lambda b, p: (p, 0, 0)),
        ],
        out_specs=pl.BlockSpec((t, LANES), lambda b, p: (b, p)),
        out_shape=jax.ShapeDtypeStruct((n * t, AT_WIDTH), F32),
        scratch_shapes=[pltpu.VMEM((t, LANES), F32) for _ in range(4)],
        compiler_params=pltpu.CompilerParams(
            dimension_semantics=("arbitrary", "arbitrary"), vmem_limit_bytes=VMEM_LIMIT),
        name="attn_prompt",
    )(at, at, at, slopes)


def _attn_sample_kernel(kt_ref, vt_ref, q_ref, kn_ref, vn_ref, slope_ref, o_ref, nkt_ref, nvt_ref,
                        *, length, hb):
    t = lax.broadcasted_iota(jnp.int32, (1, length), 1)
    newest = t == length - 1
    d = length - t
    cnt = jnp.zeros((1, length), F32)
    for dil in DILATIONS:
        hit = jnp.logical_and((d & (dil - 1)) == 0, d <= BAND * dil)
        cnt = cnt + hit.astype(F32)
    live = cnt > 0.0
    df = d.astype(F32)
    heads = range(hb)
    q = [q_ref[0, h] * (AT_HEAD_DIM ** -0.5) for h in heads]
    s = [jnp.sum(kt_ref[0, h] * q[h], axis=0, keepdims=True) for h in heads]
    s = [jnp.where(live, s[h] - slope_ref[h, 0:1, 0:1] * df, -jnp.inf) for h in heads]
    s_new = [jnp.sum(kn_ref[0, h] * q[h], axis=0, keepdims=True) for h in heads]
    mx = [jnp.maximum(jnp.max(s[h], axis=-1, keepdims=True), s_new[h]) for h in heads]
    p = [cnt * jnp.exp(s[h] - mx[h]) for h in heads]
    p_new = [float(len(DILATIONS)) * jnp.exp(s_new[h] - mx[h]) for h in heads]
    l = [jnp.sum(p[h], axis=-1, keepdims=True) + p_new[h] for h in heads]
    acc = [jnp.sum(vt_ref[0, h] * p[h], axis=-1, keepdims=True) for h in heads]
    for h in heads:
        o_ref[0, h] = (acc[h] + p_new[h] * vn_ref[0, h]) * (1.0 / l[h])
        nkt_ref[0, h] = jnp.where(newest, kn_ref[0, h], pltpu.roll(kt_ref[0, h], length - 1, 1))
        nvt_ref[0, h] = jnp.where(newest, vn_ref[0, h], pltpu.roll(vt_ref[0, h], length - 1, 1))


def _attn_sample(kt, vt, q, k_new, v_new, slopes, hb):
    n, _, _, length = kt.shape
    assert length == BAND * DILATIONS[-1] and AT_HEADS % hb == 0
    cache = pl.BlockSpec((1, hb, AT_HEAD_DIM, length), lambda b, g: (b, g, 0, 0))
    col = pl.BlockSpec((1, hb, AT_HEAD_DIM, 1), lambda b, g: (b, g, 0, 0))
    return pl.pallas_call(
        functools.partial(_attn_sample_kernel, length=length, hb=hb),
        grid=(n, AT_HEADS // hb),
        in_specs=[cache, cache, col, col, col,
                  pl.BlockSpec((hb, SUBLANES, LANES), lambda b, g: (g, 0, 0))],
        out_specs=[col, cache, cache],
        out_shape=[
            jax.ShapeDtypeStruct((n, AT_HEADS, AT_HEAD_DIM, 1), F32),
            jax.ShapeDtypeStruct(kt.shape, F32),
            jax.ShapeDtypeStruct(vt.shape, F32),
        ],
        compiler_params=pltpu.CompilerParams(
            dimension_semantics=("arbitrary", "arbitrary"), vmem_limit_bytes=VMEM_LIMIT),
        name="attn_sample",
    )(kt, vt, q, k_new, v_new, slopes)


def _outffn_kernel(x_ref, od_ref, oa_ref, wod_ref, woa_ref, ln2_ref, wup_ref, wdown_ref, lnf_ref,
                   y_ref, *, ff_chunk):
    mix = (jnp.dot(od_ref[...].astype(BF16), wod_ref[...], preferred_element_type=F32)
           + jnp.dot(oa_ref[...].astype(BF16), woa_ref[...], preferred_element_type=F32))
    x1 = x_ref[...] + mix
    h = _rmsnorm(x1, ln2_ref[...]).astype(BF16)
    acc = jnp.zeros_like(x1)
    for c in range(D_FF // ff_chunk):
        cols = slice(c * ff_chunk, (c + 1) * ff_chunk)
        up = jnp.dot(h, wup_ref[:, cols], preferred_element_type=F32)
        act = jnp.square(jnp.maximum(up, 0.0)).astype(BF16)
        acc = acc + jnp.dot(act, wdown_ref[cols, :], preferred_element_type=F32)
    y_ref[...] = _rmsnorm(x1 + acc, lnf_ref[...])


def _outffn(x, od, oa, wod, woa, ln2, wup, wdown, lnf, tm):
    m = x.shape[0]
    row = lambda i: (i, 0)
    const = lambda i: (0, 0)
    resident = functools.partial(pl.BlockSpec, index_map=const, pipeline_mode=pl.Buffered(1))
    return pl.pallas_call(
        functools.partial(_outffn_kernel, ff_chunk=1024),
        grid=(m // tm,),
        in_specs=[
            pl.BlockSpec((tm, D_MODEL), row),
            pl.BlockSpec((tm, DN_V), row),
            pl.BlockSpec((tm, AT_WIDTH), row),
            resident((DN_V, D_MODEL)),
            resident((AT_WIDTH, D_MODEL)),
            resident((1, D_MODEL)),
            resident((D_MODEL, D_FF)),
            resident((D_FF, D_MODEL)),
            resident((1, D_MODEL)),
        ],
        out_specs=pl.BlockSpec((tm, D_MODEL), row),
        out_shape=jax.ShapeDtypeStruct((m, D_MODEL), F32),
        compiler_params=pltpu.CompilerParams(
            dimension_semantics=("arbitrary",), vmem_limit_bytes=VMEM_LIMIT),
        name="outffn",
    )(x, od, oa, wod, woa, ln2, wup, wdown, lnf)


def _layer_params(ln_mix, w_in, dn_conv_w, dn_a_log, dn_dt_bias, dn_norm, w_out, ln_ffn, w_ffn_up,
                  w_ffn_down, ln_final):
    w = w_in[0]
    o_z = CONV_CH
    o_b = o_z + DN_V
    o_at = o_b + 2 * DN_HEADS
    wg = jnp.zeros((D_MODEL, LANES), F32).at[:, :2 * DN_HEADS].set(w[:, o_b:o_at])
    par = jnp.zeros((SUBLANES, LANES), F32)
    par = par.at[0, GATE_A0:GATE_A0 + DN_HEADS].set(dn_a_log[0].astype(F32))
    par = par.at[1, GATE_A0:GATE_A0 + DN_HEADS].set(dn_dt_bias[0].astype(F32))
    cw = jnp.zeros((SUBLANES, CONV_CH), F32).at[:CONV_W].set(dn_conv_w[0])
    slopes = 2.0 ** (-8.0 * jnp.arange(1, AT_HEADS + 1, dtype=F32) / AT_HEADS)
    pair_slopes = jnp.broadcast_to(
        jnp.repeat(slopes.reshape(AT_HEADS // 2, 2), AT_HEAD_DIM, axis=1)[:, None, :],
        (AT_HEADS // 2, SUBLANES, LANES))
    head_slopes = jnp.broadcast_to(slopes[:, None, None], (AT_HEADS, SUBLANES, LANES))
    return dict(
        ln_mix=ln_mix[0][None, :],
        wdn=w[:, :o_z].astype(BF16), wz=w[:, o_z:o_b].astype(BF16), wat=w[:, o_at:].astype(BF16),
        wg=wg.astype(BF16), cw=cw, par=par, nrm=dn_norm[0][None, :].astype(F32),
        pair_slopes=pair_slopes, head_slopes=head_slopes,
        wod=w_out[0][:DN_V].astype(BF16), woa=w_out[0][DN_V:].astype(BF16),
        ln_ffn=ln_ffn[0][None, :], wup=w_ffn_up[0].astype(BF16), wdown=w_ffn_down[0].astype(BF16),
        ln_final=ln_final[None, :])


def _prompt_group(x, p):
    n, t, _ = x.shape
    assert t % (DILATIONS[-1] * BAND) == 0
    x2 = x.reshape(n * t, D_MODEL)
    y, zs, gb, at, tail = _inproj_prompt(x2, p, n, t, tm=512)
    od, ssm = _gdn_prompt(y, zs, gb, p["nrm"], n, t, nc=4)
    oa = _attn_prompt(at, p["pair_slopes"], n, t)
    out = _outffn(x2, od, oa, p["wod"], p["woa"], p["ln_ffn"], p["wup"], p["wdown"], p["ln_final"],
                  tm=512)
    keep = min(WINDOW, t)
    at3 = at.reshape(n, t, 3 * AT_WIDTH)
    win_k = at3[:, t - keep:, AT_WIDTH:2 * AT_WIDTH].reshape(n, keep, AT_HEADS, AT_HEAD_DIM)
    win_v = at3[:, t - keep:, 2 * AT_WIDTH:].reshape(n, keep, AT_HEADS, AT_HEAD_DIM)
    return (out.reshape(n, t, D_MODEL), tail[:, HALO - (CONV_W - 1):][None], ssm[None],
            win_k[None], win_v[None])


def _sample_group(x, conv_st, ssm_st, k_st, v_st, p):
    n, t, _ = x.shape
    assert t == 1 and k_st.shape[2] == WINDOW
    x2 = x.reshape(n, D_MODEL)
    hist = jnp.swapaxes(conv_st[0].astype(F32), 0, 1)
    y, zs, gb, at, tail = _inproj_sample(x2, hist, p)
    od, ssm = _gdn_step(y, zs, gb, ssm_st[0].astype(F32), p["nrm"], nb=8)
    to_lanes = lambda c: jnp.transpose(c[0].astype(F32), (0, 2, 3, 1))
    from_lanes = lambda c: jnp.transpose(c, (0, 3, 1, 2))[None]
    col = lambda a: a.reshape(n, AT_HEADS, AT_HEAD_DIM, 1)
    oa, win_k, win_v = _attn_sample(
        to_lanes(k_st), to_lanes(v_st), col(at[:, :AT_WIDTH]), col(at[:, AT_WIDTH:2 * AT_WIDTH]),
        col(at[:, 2 * AT_WIDTH:]), p["head_slopes"], hb=4)
    out = _outffn(x2, od, oa.reshape(n, AT_WIDTH), p["wod"], p["woa"], p["ln_ffn"], p["wup"],
                  p["wdown"], p["ln_final"], tm=n)
    return (out.reshape(n, 1, D_MODEL), jnp.swapaxes(tail, 0, 1)[None], ssm[None],
            from_lanes(win_k), from_lanes(win_v))


def kernel(x_prompt, x_sample, state_conv, state_ssm, cache_win_k, cache_win_v, ln_mix, w_in,
           dn_conv_w, dn_a_log, dn_dt_bias, dn_norm, w_out, ln_ffn, w_ffn_up, w_ffn_down, ln_final):
    p = _layer_params(ln_mix, w_in, dn_conv_w, dn_a_log, dn_dt_bias, dn_norm, w_out, ln_ffn,
                      w_ffn_up, w_ffn_down, ln_final)
    y_p, p_conv, p_ssm, p_k, p_v = _prompt_group(x_prompt, p)
    y_s, s_conv, s_ssm, s_k, s_v = _sample_group(x_sample, state_conv, state_ssm, cache_win_k,
                                                 cache_win_v, p)
    return (y_p, y_s, p_conv, p_ssm, p_k, p_v, s_conv, s_ssm, s_k, s_v)
```

```python
import functools

import jax
import jax.numpy as jnp
from jax import lax
from jax.experimental import pallas as pl
from jax.experimental.pallas import tpu as pltpu

F32 = jnp.float32
BF16 = jnp.bfloat16

D_MODEL = 1024
DN_HEADS = 4
DN_DK = 128
DN_DV = 128
DN_QK = DN_HEADS * DN_DK
DN_V = DN_HEADS * DN_DV
CONV_CH = 2 * DN_QK + DN_V
CONV_W = 4
CHUNK = 64
AT_HEADS = 8
AT_HEAD_DIM = 64
AT_WIDTH = AT_HEADS * AT_HEAD_DIM
BAND = 128
DILATIONS = (1, 4, 16)
WINDOW = 2048
D_FF = 4 * D_MODEL
EPS = 1e-6
LOG2E = 1.4426950408889634

LANES = 128
SUBLANES = 8
HALO = SUBLANES
GATE_B0 = 0
GATE_A0 = DN_HEADS
VMEM_LIMIT = 56 * 1024 * 1024


def _rmsnorm(x, w):
    return x * lax.rsqrt(jnp.mean(x * x, axis=-1, keepdims=True) + EPS) * w


def _sigmoid(x):
    return 1.0 / (1.0 + jnp.exp(-x))


def _softplus(x):
    return jnp.maximum(x, 0.0) + jnp.log(1.0 + jnp.exp(-jnp.abs(x)))


def _mm(a, b):
    return jnp.dot(a.astype(BF16), b.astype(BF16), preferred_element_type=F32)


def _mm_nt(a, b):
    return lax.dot_general(a.astype(BF16), b.astype(BF16), (((1,), (1,)), ((), ())),
                           preferred_element_type=F32)


def _mm_tn(a, b):
    return lax.dot_general(a.astype(BF16), b.astype(BF16), (((0,), (0,)), ((), ())),
                           preferred_element_type=F32)


def _mm_exact(a, b):
    return jnp.dot(a, b, precision=lax.Precision.HIGHEST, preferred_element_type=F32)


def _store_mixer_inputs(conv, y_ref):
    y = conv * _sigmoid(conv)
    for blk in range(2 * DN_HEADS):
        cols = slice(blk * DN_DK, (blk + 1) * DN_DK)
        part = y[:, cols]
        part = part * lax.rsqrt(jnp.sum(part * part, axis=-1, keepdims=True) + EPS)
        if blk < DN_HEADS:
            part = part * (DN_DK ** -0.5)
        y_ref[:, cols] = part
    y_ref[:, 2 * DN_QK:] = y[:, 2 * DN_QK:]


def _store_gates(gate, par_ref, gb_ref):
    lane = lax.broadcasted_iota(jnp.int32, (1, LANES), 1)
    beta = _sigmoid(gate)
    g = -jnp.exp(par_ref[0:1, :]) * _softplus(gate + par_ref[1:2, :])
    gb_ref[...] = jnp.where(lane < GATE_A0, beta, g)


def _inproj_prompt_kernel(x_ref, ln_ref, wdn_ref, wz_ref, wat_ref, wg_ref, cw_ref, par_ref,
                          y_ref, zs_ref, gb_ref, at_ref, tail_ref, xc_ref, *, tm, seq_tiles):
    j = lax.rem(pl.program_id(0), seq_tiles)

    @pl.when(pl.program_id(0) == 0)
    def _():
        xc_ref[tm:tm + HALO, :] = jnp.zeros((HALO, CONV_CH), F32)

    h = _rmsnorm(x_ref[...], ln_ref[...]).astype(BF16)

    xc_ref[0:HALO, :] = jnp.where(j == 0, 0.0, xc_ref[tm:tm + HALO, :])
    xc_ref[HALO:HALO + tm, :] = jnp.dot(h, wdn_ref[...], preferred_element_type=F32)
    tail_ref[0] = xc_ref[tm:tm + HALO, :]

    lo = HALO - (CONV_W - 1)
    conv = xc_ref[lo:lo + tm, :] * cw_ref[0:1, :]
    for i in range(1, CONV_W):
        conv = conv + xc_ref[lo + i:lo + i + tm, :] * cw_ref[i:i + 1, :]
    _store_mixer_inputs(conv, y_ref)

    at_ref[...] = jnp.dot(h, wat_ref[...], preferred_element_type=F32)
    z = jnp.dot(h, wz_ref[...], preferred_element_type=F32)
    zs_ref[...] = z * _sigmoid(z)
    _store_gates(jnp.dot(h, wg_ref[...], preferred_element_type=F32), par_ref, gb_ref)


def _inproj_sample_kernel(x_ref, hist_ref, ln_ref, wdn_ref, wz_ref, wat_ref, wg_ref, cw_ref, par_ref,
                          y_ref, zs_ref, gb_ref, at_ref, tail_ref):
    h = _rmsnorm(x_ref[...], ln_ref[...]).astype(BF16)
    dn = jnp.dot(h, wdn_ref[...], preferred_element_type=F32)
    z = jnp.dot(h, wz_ref[...], preferred_element_type=F32)
    zs_ref[...] = z * _sigmoid(z)
    at_ref[...] = jnp.dot(h, wat_ref[...], preferred_element_type=F32)
    gate = jnp.dot(h, wg_ref[...], preferred_element_type=F32)

    conv = hist_ref[0] * cw_ref[0:1, :]
    for i in range(1, CONV_W - 1):
        conv = conv + hist_ref[i] * cw_ref[i:i + 1, :]
    conv = conv + dn * cw_ref[CONV_W - 1:CONV_W, :]
    for i in range(1, CONV_W - 1):
        tail_ref[i - 1] = hist_ref[i]
    tail_ref[CONV_W - 2] = dn
    _store_mixer_inputs(conv, y_ref)
    _store_gates(gate, par_ref, gb_ref)


def _weight_specs():
    const = lambda i: (0, 0)
    return [
        pl.BlockSpec((1, D_MODEL), const),
        pl.BlockSpec((D_MODEL, CONV_CH), const),
        pl.BlockSpec((D_MODEL, DN_V), const),
        pl.BlockSpec((D_MODEL, 3 * AT_WIDTH), const),
        pl.BlockSpec((D_MODEL, LANES), const),
        pl.BlockSpec((SUBLANES, CONV_CH), const),
        pl.BlockSpec((SUBLANES, LANES), const),
    ]


def _inproj_prompt(x, p, n, t, tm):
    m = n * t
    seq_tiles = t // tm
    row = lambda i: (i, 0)
    return pl.pallas_call(
        functools.partial(_inproj_prompt_kernel, tm=tm, seq_tiles=seq_tiles),
        grid=(m // tm,),
        in_specs=[pl.BlockSpec((tm, D_MODEL), row)] + _weight_specs(),
        out_specs=[
            pl.BlockSpec((tm, CONV_CH), row),
            pl.BlockSpec((tm, DN_V), row),
            pl.BlockSpec((tm, LANES), row),
            pl.BlockSpec((tm, 3 * AT_WIDTH), row),
            pl.BlockSpec((1, HALO, CONV_CH), lambda i: (i // seq_tiles, 0, 0)),
        ],
        out_shape=[
            jax.ShapeDtypeStruct((m, CONV_CH), F32),
            jax.ShapeDtypeStruct((m, DN_V), F32),
            jax.ShapeDtypeStruct((m, LANES), F32),
            jax.ShapeDtypeStruct((m, 3 * AT_WIDTH), F32),
            jax.ShapeDtypeStruct((n, HALO, CONV_CH), F32),
        ],
        scratch_shapes=[pltpu.VMEM((HALO + tm, CONV_CH), F32)],
        compiler_params=pltpu.CompilerParams(
            dimension_semantics=("arbitrary",), vmem_limit_bytes=VMEM_LIMIT),
        name="inproj_prompt",
    )(x, p["ln_mix"], p["wdn"], p["wz"], p["wat"], p["wg"], p["cw"], p["par"])


def _inproj_sample(x, hist, p):
    n = x.shape[0]
    full = lambda i: (0, 0)
    full3 = lambda i: (0, 0, 0)
    return pl.pallas_call(
        _inproj_sample_kernel,
        grid=(1,),
        in_specs=[pl.BlockSpec((n, D_MODEL), full),
                  pl.BlockSpec((CONV_W - 1, n, CONV_CH), full3)] + _weight_specs(),
        out_specs=[
            pl.BlockSpec((n, CONV_CH), full),
            pl.BlockSpec((n, DN_V), full),
            pl.BlockSpec((n, LANES), full),
            pl.BlockSpec((n, 3 * AT_WIDTH), full),
            pl.BlockSpec((CONV_W - 1, n, CONV_CH), full3),
        ],
        out_shape=[
            jax.ShapeDtypeStruct((n, CONV_CH), F32),
            jax.ShapeDtypeStruct((n, DN_V), F32),
            jax.ShapeDtypeStruct((n, LANES), F32),
            jax.ShapeDtypeStruct((n, 3 * AT_WIDTH), F32),
            jax.ShapeDtypeStruct((CONV_W - 1, n, CONV_CH), F32),
        ],
        compiler_params=pltpu.CompilerParams(
            dimension_semantics=("arbitrary",), vmem_limit_bytes=VMEM_LIMIT),
        name="inproj_sample",
    )(x, hist, p["ln_mix"], p["wdn"], p["wz"], p["wat"], p["wg"], p["cw"], p["par"])


def _unit_lower_inverses(ms, eye, same16, same32):
    n = [jnp.where(same16, m, 0.0) for m in ms]
    p = [eye - a for a in n]
    n = [_mm(a, a) for a in n]
    for _ in range(2):
        both = [_mm(jnp.concatenate([a, b], axis=0), b) for a, b in zip(p, n)]
        p = [a + b[:CHUNK] for a, b in zip(p, both)]
        n = [b[CHUNK:] for b in both]
    p = [a + _mm(a, b) for a, b in zip(p, n)]
    off16 = jnp.logical_and(same32, jnp.logical_not(same16))
    pe = [_mm(a, jnp.where(off16, m, 0.0)) for a, m in zip(p, ms)]
    t = [a - _mm(b, a) for a, b in zip(p, pe)]
    te = [_mm(a, jnp.where(same32, 0.0, m)) for a, m in zip(t, ms)]
    return [a - _mm(b, a) for a, b in zip(t, te)]


def _gated_norm(o, zs, nrm):
    return o * lax.rsqrt(jnp.mean(o * o, axis=-1, keepdims=True) + EPS) * nrm * zs


def _gdn_kernel(y_ref, zs_ref, gb_ref, nrm_ref, od_ref, s_ref, *, nc, ns):
    tc = nc * CHUNK

    @pl.when(pl.program_id(1) == 0)
    def _():
        s_ref[...] = jnp.zeros(s_ref.shape, F32)

    ri = lax.broadcasted_iota(jnp.int32, (tc, tc), 0)
    ci = lax.broadcasted_iota(jnp.int32, (tc, tc), 1)
    lblk = jnp.logical_and((ri >> 6) == (ci >> 6), ci <= ri).astype(F32)
    gb = [gb_ref[b] for b in range(ns)]
    gc_all = [_mm_exact(lblk, g) for g in gb]
    gc_t = [g.T for g in gc_all]

    r64 = lax.broadcasted_iota(jnp.int32, (CHUNK, CHUNK), 0)
    c64 = lax.broadcasted_iota(jnp.int32, (CHUNK, CHUNK), 1)
    incl = c64 <= r64
    strict = c64 < r64
    eye = (c64 == r64).astype(F32)
    same16 = (r64 >> 4) == (c64 >> 4)
    same32 = (r64 >> 5) == (c64 >> 5)
    nrm = nrm_ref[...]

    cells = [(b, c, h) for b in range(ns) for c in range(nc) for h in range(DN_HEADS)]
    qn, kn, kb, vb, gcc, egc, gl, decay = [], [], [], [], [], [], [], []
    for b, c, h in cells:
        rs = slice(c * CHUNK, (c + 1) * CHUNK)
        beta = gb[b][rs, GATE_B0 + h:GATE_B0 + h + 1]
        col = gc_all[b][rs, GATE_A0 + h:GATE_A0 + h + 1]
        row = gc_t[b][GATE_A0 + h:GATE_A0 + h + 1, rs]
        qn.append(y_ref[b, rs, h * DN_DK:(h + 1) * DN_DK])
        kn.append(y_ref[b, rs, DN_QK + h * DN_DK:DN_QK + (h + 1) * DN_DK])
        kb.append(kn[-1] * beta)
        vb.append(y_ref[b, rs, 2 * DN_QK + h * DN_DV:2 * DN_QK + (h + 1) * DN_DV] * beta)
        gcc.append(col)
        egc.append(jnp.exp(col))
        gl.append(gc_all[b][(c + 1) * CHUNK - 1:(c + 1) * CHUNK, GATE_A0 + h:GATE_A0 + h + 1])
        decay.append(jnp.exp(jnp.minimum(col - row, 0.0)))

    kq = [_mm_nt(jnp.concatenate([a, b], axis=0), k) for a, b, k in zip(kb, qn, kn)]
    ms = [jnp.where(strict, a[:CHUNK] * d, 0.0) for a, d in zip(kq, decay)]
    qk = [jnp.where(incl, a[CHUNK:] * d, 0.0) for a, d in zip(kq, decay)]
    ts = _unit_lower_inverses(ms, eye, same16, same32)
    sol = [_mm(t, jnp.concatenate([a, b * e], axis=1))
           for t, a, b, e in zip(ts, vb, kb, egc)]

    lanes = [(b, h) for b in range(ns) for h in range(DN_HEADS)]
    s = [s_ref[b, h] for b, h in lanes]
    for c in range(nc):
        ids = [(b * nc + c) * DN_HEADS + h for b, h in lanes]
        wq = [_mm(jnp.concatenate([sol[i][:, DN_DV:], qn[i] * egc[i]], axis=0), s[j])
              for j, i in enumerate(ids)]
        v_new = [sol[i][:, :DN_DV] - a[:CHUNK] for i, a in zip(ids, wq)]
        intra = [_mm(qk[i], v) for i, v in zip(ids, v_new)]
        upd = [_mm_tn(kn[i] * jnp.exp(gl[i] - gcc[i]), v) for i, v in zip(ids, v_new)]
        s = [s[j] * jnp.exp(gl[i]) + upd[j] for j, i in enumerate(ids)]
        rs = slice(c * CHUNK, (c + 1) * CHUNK)
        for j, (b, h) in enumerate(lanes):
            cols = slice(h * DN_DV, (h + 1) * DN_DV)
            od_ref[b, rs, cols] = _gated_norm(wq[j][CHUNK:] + intra[j], zs_ref[b, rs, cols], nrm)
    for j, (b, h) in enumerate(lanes):
        s_ref[b, h] = s[j]


def _gdn_prompt(y, zs, gb, nrm, n, t, nc, ns):
    tc = nc * CHUNK
    assert t % tc == 0 and n % ns == 0
    tile = lambda b, j: (b, j, 0)
    state = pl.BlockSpec((ns, DN_HEADS, DN_DK, DN_DV), lambda b, j: (b, 0, 0, 0))
    return pl.pallas_call(
        functools.partial(_gdn_kernel, nc=nc, ns=ns),
        grid=(n // ns, t // tc),
        in_specs=[
            pl.BlockSpec((ns, tc, CONV_CH), tile),
            pl.BlockSpec((ns, tc, DN_V), tile),
            pl.BlockSpec((ns, tc, LANES), tile),
            pl.BlockSpec((1, DN_DV), lambda b, j: (0, 0)),
        ],
        out_specs=[pl.BlockSpec((ns, tc, DN_V), tile), state],
        out_shape=[
            jax.ShapeDtypeStruct((n, t, DN_V), F32),
            jax.ShapeDtypeStruct((n, DN_HEADS, DN_DK, DN_DV), F32),
        ],
        compiler_params=pltpu.CompilerParams(
            dimension_semantics=("arbitrary", "arbitrary"), vmem_limit_bytes=VMEM_LIMIT),
        name="gdn_prompt",
    )(y.reshape(n, t, CONV_CH), zs.reshape(n, t, DN_V), gb.reshape(n, t, LANES), nrm)


def _gdn_step_kernel(y_ref, zs_ref, gb_ref, s0_ref, nrm_ref, od_ref, s_ref, *, nb):
    y = y_ref[...]
    zs = zs_ref[...]
    gb = gb_ref[...]
    nrm = nrm_ref[...]
    rowid = lax.broadcasted_iota(jnp.int32, (nb, DN_DK), 0)
    cells = [(b, h) for b in range(nb) for h in range(DN_HEADS)]
    qh = [y[:, h * DN_DK:(h + 1) * DN_DK] for h in range(DN_HEADS)]
    kh = [y[:, DN_QK + h * DN_DK:DN_QK + (h + 1) * DN_DK] for h in range(DN_HEADS)]
    vh = [y[:, 2 * DN_QK + h * DN_DV:2 * DN_QK + (h + 1) * DN_DV] for h in range(DN_HEADS)]
    ks = [_mm(kh[h], s0_ref[b, h])[b:b + 1] for b, h in cells]
    qs = [_mm(qh[h], s0_ref[b, h])[b:b + 1] for b, h in cells]
    rs, egs = [], []
    for i, (b, h) in enumerate(cells):
        beta = gb[b:b + 1, GATE_B0 + h:GATE_B0 + h + 1]
        eg = jnp.exp(gb[b:b + 1, GATE_A0 + h:GATE_A0 + h + 1])
        rs.append(beta * (vh[h][b:b + 1] - eg * ks[i]))
        egs.append(eg)
    upd = [_mm_tn(jnp.where(rowid == b, kh[h], 0.0), jnp.broadcast_to(rs[i], (nb, DN_DV)))
           for i, (b, h) in enumerate(cells)]
    for i, (b, h) in enumerate(cells):
        s_ref[b, h] = s0_ref[b, h] * egs[i] + upd[i]
        qk = jnp.sum(qh[h][b:b + 1] * kh[h][b:b + 1], axis=-1, keepdims=True)
        cols = slice(h * DN_DV, (h + 1) * DN_DV)
        od_ref[b:b + 1, cols] = _gated_norm(egs[i] * qs[i] + qk * rs[i], zs[b:b + 1, cols], nrm)


def _gdn_step(y, zs, gb, s0, nrm, nb):
    n = y.shape[0]
    assert n % nb == 0
    row = lambda i: (i, 0)
    state = pl.BlockSpec((nb, DN_HEADS, DN_DK, DN_DV), lambda i: (i, 0, 0, 0))
    return pl.pallas_call(
        functools.partial(_gdn_step_kernel, nb=nb),
        grid=(n // nb,),
        in_specs=[
            pl.BlockSpec((nb, CONV_CH), row),
            pl.BlockSpec((nb, DN_V), row),
            pl.BlockSpec((nb, LANES), row),
            state,
            pl.BlockSpec((1, DN_DV), lambda i: (0, 0)),
        ],
        out_specs=[pl.BlockSpec((nb, DN_V), row), state],
        out_shape=[
            jax.ShapeDtypeStruct((n, DN_V), F32),
            jax.ShapeDtypeStruct((n, DN_HEADS, DN_DK, DN_DV), F32),
        ],
        compiler_params=pltpu.CompilerParams(
            dimension_semantics=("arbitrary",), vmem_limit_bytes=VMEM_LIMIT),
        name="gdn_step",
    )(y, zs, gb, s0, nrm)


def _attn_runs(q_ref, k_ref, v_ref, bias_ref, runs, g, dil, is_a):
    def rows(ref, r, blk):
        at = dil * BAND * blk + r
        if dil == 1:
            return ref[pl.ds(at, BAND), :]
        return ref[pl.ds(at, BAND, stride=dil), :]

    qs, k2, v2, firsts = [], [], [], []
    for r, b0 in runs:
        kc = [rows(k_ref, r, jnp.maximum(b0 - 1, 0)).astype(BF16)]
        vc = [rows(v_ref, r, jnp.maximum(b0 - 1, 0)).astype(BF16)]
        for i in range(g):
            kc.append(rows(k_ref, r, b0 + i).astype(BF16))
            vc.append(rows(v_ref, r, b0 + i).astype(BF16))
            qs.append(rows(q_ref, r, b0 + i) * (LOG2E * AT_HEAD_DIM ** -0.5))
            k2.append(jnp.concatenate([kc[i], kc[i + 1]], axis=0))
            v2.append(jnp.concatenate([vc[i], vc[i + 1]], axis=0))
            firsts.append((b0 == 0).astype(jnp.int32) if i == 0 else None)

    heads = [(i, a) for i in range(len(qs)) for a in (0, 1)]
    s = [_mm_nt(jnp.where(is_a, 0.0, qs[i]) if a else jnp.where(is_a, qs[i], 0.0), k2[i])
         for i, a in heads]
    s = [x + (bias_ref[2 * a] if firsts[i] is None else bias_ref[2 * a + firsts[i]])
         for x, (i, a) in zip(s, heads)]
    mx = [jnp.max(x, axis=-1, keepdims=True) for x in s]
    p = [jnp.exp2(x - m) for x, m in zip(s, mx)]
    l = [jnp.sum(x, axis=-1, keepdims=True) for x in p]
    pv = [jnp.dot(x.astype(BF16), v2[i], preferred_element_type=F32)
          for x, (i, a) in zip(p, heads)]
    outs = []
    for i in range(len(qs)):
        a, b = 2 * i, 2 * i + 1
        o = jnp.where(is_a, pv[a] * (1.0 / l[a]), pv[b] * (1.0 / l[b]))
        lse = jnp.where(is_a, mx[a] + jnp.log2(l[a]), mx[b] + jnp.log2(l[b]))
        outs.append((o, lse))
    return outs


def _attn_kernel(q_ref, k_ref, v_ref, slope_ref, o_ref, o16_ref, l16_ref, o4_ref, l4_ref, bias_ref,
                 *, t, group):
    is_a = lax.broadcasted_iota(jnp.int32, (BAND, LANES), 1) < AT_HEAD_DIM
    ri = lax.broadcasted_iota(jnp.int32, (BAND, 2 * BAND), 0)
    cj = lax.broadcasted_iota(jnp.int32, (BAND, 2 * BAND), 1)
    dist = ri + BAND - cj
    distf = dist.astype(F32) * LOG2E
    band = jnp.logical_and(dist >= 0, dist <= BAND)
    band_first = jnp.logical_and(band, cj >= BAND)
    slopes = (slope_ref[0, 0:1, 0:1], slope_ref[0, 0:1, AT_HEAD_DIM:AT_HEAD_DIM + 1])

    for dil, on_ref, ls_ref in ((16, o16_ref, l16_ref), (4, o4_ref, l4_ref), (1, None, None)):
        nb = t // (dil * BAND)
        g = min(group, nb)
        runs_per_step = group // g
        runs_per_res = nb // g
        shift = runs_per_res.bit_length() - 1
        for a in (0, 1):
            alibi = -(slopes[a] * float(dil)) * distf
            bias_ref[2 * a] = jnp.where(band, alibi, -jnp.inf)
            bias_ref[2 * a + 1] = jnp.where(band_first, alibi, -jnp.inf)

        def body(it, carry, dil=dil, g=g, runs_per_step=runs_per_step, runs_per_res=runs_per_res,
                 shift=shift, on_ref=on_ref, ls_ref=ls_ref):
            ids = [it * runs_per_step + u for u in range(runs_per_step)]
            runs = [(i >> shift, (i & (runs_per_res - 1)) * g) for i in ids]
            outs = _attn_runs(q_ref, k_ref, v_ref, bias_ref, runs, g, dil, is_a)
            starts = [dil * BAND * (b0 + i) + r for r, b0 in runs for i in range(g)]
            for start, (o, lse) in zip(starts, outs):
                if dil > 1:
                    on_ref[pl.ds(start, BAND, stride=dil), :] = o
                    ls_ref[pl.ds(start, BAND, stride=dil), :] = lse
                else:
                    rows = pl.ds(start, BAND)
                    l16 = l16_ref[rows, :]
                    l4 = l4_ref[rows, :]
                    mx = jnp.maximum(jnp.maximum(lse, l16), l4)
                    w1 = jnp.exp2(lse - mx)
                    w16 = jnp.exp2(l16 - mx)
                    w4 = jnp.exp2(l4 - mx)
                    num = w1 * o + w16 * o16_ref[rows, :] + w4 * o4_ref[rows, :]
                    o_ref[rows, :] = num * (1.0 / (w1 + w16 + w4))
            return carry

        lax.fori_loop(0, dil * nb // (g * runs_per_step), body, 0)


def _attn_prompt(at, slopes, n, t, group=4):
    assert (t // BAND) % group == 0
    kern = functools.partial(_attn_kernel, t=t, group=group)
    pairs = AT_HEADS // 2
    return pl.pallas_call(
        kern,
        grid=(n, pairs),
        in_specs=[
            pl.BlockSpec((t, LANES), lambda b, p: (b, p)),
            pl.BlockSpec((t, LANES), lambda b, p: (b, pairs + p)),
            pl.BlockSpec((t, LANES), lambda b, p: (b, 2 * pairs + p)),
            pl.BlockSpec((1, SUBLANES, LANES), lambda b, p: (p, 0, 0)),
        ],
        out_specs=pl.BlockSpec((t, LANES), lambda b, p: (b, p)),
        out_shape=jax.ShapeDtypeStruct((n * t, AT_WIDTH), F32),
        scratch_shapes=[pltpu.VMEM((t, LANES), F32) for _ in range(4)]
        + [pltpu.VMEM((4, BAND, 2 * BAND), F32)],
        compiler_params=pltpu.CompilerParams(
            dimension_semantics=("arbitrary", "arbitrary"), vmem_limit_bytes=VMEM_LIMIT),
        name="attn_prompt",
    )(at, at, at, slopes)


def _attn_sample_kernel(kt_ref, vt_ref, q_ref, kn_ref, vn_ref, slope_ref, o_ref, nkt_ref, nvt_ref,
                        *, length, hb):
    t = lax.broadcasted_iota(jnp.int32, (1, length), 1)
    newest = t == length - 1
    d = length - t
    cnt = jnp.zeros((1, length), F32)
    for dil in DILATIONS:
        hit = jnp.logical_and((d & (dil - 1)) == 0, d <= BAND * dil)
        cnt = cnt + hit.astype(F32)
    live = cnt > 0.0
    df = d.astype(F32)
    heads = range(hb)
    q = [q_ref[0, h] * (AT_HEAD_DIM ** -0.5) for h in heads]
    s = [jnp.sum(kt_ref[0, h] * q[h], axis=0, keepdims=True) for h in heads]
    s = [jnp.where(live, s[h] - slope_ref[h, 0:1, 0:1] * df, -jnp.inf) for h in heads]
    s_new = [jnp.sum(kn_ref[0, h] * q[h], axis=0, keepdims=True) for h in heads]
    mx = [jnp.maximum(jnp.max(s[h], axis=-1, keepdims=True), s_new[h]) for h in heads]
    p = [cnt * jnp.exp(s[h] - mx[h]) for h in heads]
    p_new = [float(len(DILATIONS)) * jnp.exp(s_new[h] - mx[h]) for h in heads]
    l = [jnp.sum(p[h], axis=-1, keepdims=True) + p_new[h] for h in heads]
    acc = [jnp.sum(vt_ref[0, h] * p[h], axis=-1, keepdims=True) for h in heads]
    for h in heads:
        o_ref[0, h] = (acc[h] + p_new[h] * vn_ref[0, h]) * (1.0 / l[h])
        nkt_ref[0, h] = jnp.where(newest, kn_ref[0, h], pltpu.roll(kt_ref[0, h], length - 1, 1))
        nvt_ref[0, h] = jnp.where(newest, vn_ref[0, h], pltpu.roll(vt_ref[0, h], length - 1, 1))


def _attn_sample(kt, vt, q, k_new, v_new, slopes, hb):
    n, _, _, length = kt.shape
    assert length == BAND * DILATIONS[-1] and AT_HEADS % hb == 0
    cache = pl.BlockSpec((1, hb, AT_HEAD_DIM, length), lambda b, g: (b, g, 0, 0))
    col = pl.BlockSpec((1, hb, AT_HEAD_DIM, 1), lambda b, g: (b, g, 0, 0))
    return pl.pallas_call(
        functools.partial(_attn_sample_kernel, length=length, hb=hb),
        grid=(n, AT_HEADS // hb),
        in_specs=[cache, cache, col, col, col,
                  pl.BlockSpec((hb, SUBLANES, LANES), lambda b, g: (g, 0, 0))],
        out_specs=[col, cache, cache],
        out_shape=[
            jax.ShapeDtypeStruct((n, AT_HEADS, AT_HEAD_DIM, 1), F32),
            jax.ShapeDtypeStruct(kt.shape, F32),
            jax.ShapeDtypeStruct(vt.shape, F32),
        ],
        compiler_params=pltpu.CompilerParams(
            dimension_semantics=("arbitrary", "arbitrary"), vmem_limit_bytes=VMEM_LIMIT),
        name="attn_sample",
    )(kt, vt, q, k_new, v_new, slopes)


def _outffn_kernel(x_ref, od_ref, oa_ref, wod_ref, woa_ref, ln2_ref, wup_ref, wdown_ref, lnf_ref,
                   y_ref, *, ff_chunk):
    mix = (jnp.dot(od_ref[...].astype(BF16), wod_ref[...], preferred_element_type=F32)
           + jnp.dot(oa_ref[...].astype(BF16), woa_ref[...], preferred_element_type=F32))
    x1 = x_ref[...] + mix
    h = _rmsnorm(x1, ln2_ref[...]).astype(BF16)
    acc = jnp.zeros_like(x1)
    for c in range(D_FF // ff_chunk):
        cols = slice(c * ff_chunk, (c + 1) * ff_chunk)
        up = jnp.dot(h, wup_ref[:, cols], preferred_element_type=F32)
        act = jnp.square(jnp.maximum(up, 0.0)).astype(BF16)
        acc = acc + jnp.dot(act, wdown_ref[cols, :], preferred_element_type=F32)
    y_ref[...] = _rmsnorm(x1 + acc, lnf_ref[...])


def _outffn(x, od, oa, wod, woa, ln2, wup, wdown, lnf, tm):
    m = x.shape[0]
    row = lambda i: (i, 0)
    const = lambda i: (0, 0)
    resident = functools.partial(pl.BlockSpec, index_map=const, pipeline_mode=pl.Buffered(1))
    return pl.pallas_call(
        functools.partial(_outffn_kernel, ff_chunk=1024),
        grid=(m // tm,),
        in_specs=[
            pl.BlockSpec((tm, D_MODEL), row),
            pl.BlockSpec((tm, DN_V), row),
            pl.BlockSpec((tm, AT_WIDTH), row),
            resident((DN_V, D_MODEL)),
            resident((AT_WIDTH, D_MODEL)),
            resident((1, D_MODEL)),
            resident((D_MODEL, D_FF)),
            resident((D_FF, D_MODEL)),
            resident((1, D_MODEL)),
        ],
        out_specs=pl.BlockSpec((tm, D_MODEL), row),
        out_shape=jax.ShapeDtypeStruct((m, D_MODEL), F32),
        compiler_params=pltpu.CompilerParams(
            dimension_semantics=("arbitrary",), vmem_limit_bytes=VMEM_LIMIT),
        name="outffn",
    )(x, od, oa, wod, woa, ln2, wup, wdown, lnf)


def _layer_params(ln_mix, w_in, dn_conv_w, dn_a_log, dn_dt_bias, dn_norm, w_out, ln_ffn, w_ffn_up,
                  w_ffn_down, ln_final):
    w = w_in[0]
    o_z = CONV_CH
    o_b = o_z + DN_V
    o_at = o_b + 2 * DN_HEADS
    wg = jnp.zeros((D_MODEL, LANES), F32).at[:, :2 * DN_HEADS].set(w[:, o_b:o_at])
    par = jnp.zeros((SUBLANES, LANES), F32)
    par = par.at[0, GATE_A0:GATE_A0 + DN_HEADS].set(dn_a_log[0].astype(F32))
    par = par.at[1, GATE_A0:GATE_A0 + DN_HEADS].set(dn_dt_bias[0].astype(F32))
    cw = jnp.zeros((SUBLANES, CONV_CH), F32).at[:CONV_W].set(dn_conv_w[0])
    slopes = 2.0 ** (-8.0 * jnp.arange(1, AT_HEADS + 1, dtype=F32) / AT_HEADS)
    pair_slopes = jnp.broadcast_to(
        jnp.repeat(slopes.reshape(AT_HEADS // 2, 2), AT_HEAD_DIM, axis=1)[:, None, :],
        (AT_HEADS // 2, SUBLANES, LANES))
    head_slopes = jnp.broadcast_to(slopes[:, None, None], (AT_HEADS, SUBLANES, LANES))
    return dict(
        ln_mix=ln_mix[0][None, :],
        wdn=w[:, :o_z].astype(BF16), wz=w[:, o_z:o_b].astype(BF16), wat=w[:, o_at:].astype(BF16),
        wg=wg.astype(BF16), cw=cw, par=par, nrm=dn_norm[0][None, :].astype(F32),
        pair_slopes=pair_slopes, head_slopes=head_slopes,
        wod=w_out[0][:DN_V].astype(BF16), woa=w_out[0][DN_V:].astype(BF16),
        ln_ffn=ln_ffn[0][None, :], wup=w_ffn_up[0].astype(BF16), wdown=w_ffn_down[0].astype(BF16),
        ln_final=ln_final[None, :])


def _prompt_group(x, p):
    n, t, _ = x.shape
    assert t % (DILATIONS[-1] * BAND) == 0
    x2 = x.reshape(n * t, D_MODEL)
    y, zs, gb, at, tail = _inproj_prompt(x2, p, n, t, tm=512)
    od, ssm = _gdn_prompt(y, zs, gb, p["nrm"], n, t, nc=4, ns=2)
    od = od.reshape(n * t, DN_V)
    oa = _attn_prompt(at, p["pair_slopes"], n, t)
    out = _outffn(x2, od, oa, p["wod"], p["woa"], p["ln_ffn"], p["wup"], p["wdown"], p["ln_final"],
                  tm=512)
    keep = min(WINDOW, t)
    at3 = at.reshape(n, t, 3 * AT_WIDTH)
    win_k = at3[:, t - keep:, AT_WIDTH:2 * AT_WIDTH].reshape(n, keep, AT_HEADS, AT_HEAD_DIM)
    win_v = at3[:, t - keep:, 2 * AT_WIDTH:].reshape(n, keep, AT_HEADS, AT_HEAD_DIM)
    return (out.reshape(n, t, D_MODEL), tail[:, HALO - (CONV_W - 1):][None], ssm[None],
            win_k[None], win_v[None])


def _sample_group(x, conv_st, ssm_st, k_st, v_st, p):
    n, t, _ = x.shape
    assert t == 1 and k_st.shape[2] == WINDOW
    x2 = x.reshape(n, D_MODEL)
    hist = jnp.swapaxes(conv_st[0].astype(F32), 0, 1)
    y, zs, gb, at, tail = _inproj_sample(x2, hist, p)
    od, ssm = _gdn_step(y, zs, gb, ssm_st[0].astype(F32), p["nrm"], nb=8)
    to_lanes = lambda c: jnp.transpose(c[0].astype(F32), (0, 2, 3, 1))
    from_lanes = lambda c: jnp.transpose(c, (0, 3, 1, 2))[None]
    col = lambda a: a.reshape(n, AT_HEADS, AT_HEAD_DIM, 1)
    oa, win_k, win_v = _attn_sample(
        to_lanes(k_st), to_lanes(v_st), col(at[:, :AT_WIDTH]), col(at[:, AT_WIDTH:2 * AT_WIDTH]),
        col(at[:, 2 * AT_WIDTH:]), p["head_slopes"], hb=4)
    out = _outffn(x2, od, oa.reshape(n, AT_WIDTH), p["wod"], p["woa"], p["ln_ffn"], p["wup"],
                  p["wdown"], p["ln_final"], tm=n)
    return (out.reshape(n, 1, D_MODEL), jnp.swapaxes(tail, 0, 1)[None], ssm[None],
            from_lanes(win_k), from_lanes(win_v))


def kernel(x_prompt, x_sample, state_conv, state_ssm, cache_win_k, cache_win_v, ln_mix, w_in,
           dn_conv_w, dn_a_log, dn_dt_bias, dn_norm, w_out, ln_ffn, w_ffn_up, w_ffn_down, ln_final):
    p = _layer_params(ln_mix, w_in, dn_conv_w, dn_a_log, dn_dt_bias, dn_norm, w_out, ln_ffn,
                      w_ffn_up, w_ffn_down, ln_final)
    y_p, p_conv, p_ssm, p_k, p_v = _prompt_group(x_prompt, p)
    y_s, s_conv, s_ssm, s_k, s_v = _sample_group(x_sample, state_conv, state_ssm, cache_win_k,
                                                 cache_win_v, p)
    return (y_p, y_s, p_conv, p_ssm, p_k, p_v, s_conv, s_ssm, s_k, s_v)
```
